```python
import jax, jax.numpy as jnp
from jax import lax
import numpy as np

D_MODEL = 1024
BATCH = 4
SEQ = 4096
DEPTH = 2
DEC_BATCH = 2
DEC_SEQ = 8192
PAST_LEN = 128

EPS = 1e-6
HEAD_DIM = 64
A_HEADS = 8
A_KV_HEADS = 2
A_WINDOW = 128
A_BLOCK = 128
R_HEADS = 4
R_DK = 128
R_DV = 128
R_CHUNK = 128
N_HEADS = 8
GRID_W = 64
NA_ROWS_MAX = 8
NA_COLS = 16
MEM_LEN = 256
X_HEADS = 4
X_HEAD_DIM = D_MODEL // X_HEADS
D_FF_RAW = -(-8 * D_MODEL // 3)
D_FF = 256 * (-(-D_FF_RAW // 256))

A_Q = A_HEADS * HEAD_DIM
A_KV = A_KV_HEADS * HEAD_DIM
R_QK = R_HEADS * R_DK
R_V = R_HEADS * R_DV
N_W = N_HEADS * HEAD_DIM
BRANCH_W = 512
N_BRANCH = 3
SPLITS = (A_Q, A_KV, A_KV, R_QK, R_QK, R_V, R_V, N_W, N_W, N_W, D_MODEL, D_MODEL, D_MODEL)
D_IN = sum(SPLITS)

kernel_name = "hybrid_bidir_encoder_gqa_retention_natten"


def rms_norm(x, g):
    xf = x.astype(jnp.float32)
    y = xf * lax.rsqrt(jnp.mean(xf * xf, axis=-1, keepdims=True) + EPS)
    return (y * g.astype(jnp.float32)).astype(x.dtype)


def alibi_slopes(n):
    return jnp.exp2(-8.0 * jnp.arange(1, n + 1, dtype=jnp.float32) / n)


def window_gqa(q, k, v, sink):
    B, L, _, d = q.shape
    nb = L // A_BLOCK
    G = A_HEADS // A_KV_HEADS
    pad = ((0, 0), (A_BLOCK, A_BLOCK), (0, 0), (0, 0))
    kp = jnp.pad(k, pad).reshape(B, nb + 2, A_BLOCK, A_KV_HEADS, d)
    vp = jnp.pad(v, pad).reshape(B, nb + 2, A_BLOCK, A_KV_HEADS, d)
    kb = jnp.concatenate([kp[:, :-2], kp[:, 1:-1], kp[:, 2:]], axis=2)
    vb = jnp.concatenate([vp[:, :-2], vp[:, 1:-1], vp[:, 2:]], axis=2)
    qb = q.reshape(B, nb, A_BLOCK, A_KV_HEADS, G, d)
    s = jnp.einsum('bnqhgd,bnkhd->bnhgqk', qb, kb).astype(jnp.float32) * (d ** -0.5)
    i = jnp.arange(A_BLOCK)[:, None]
    j = jnp.arange(3 * A_BLOCK)[None, :]
    dist = jnp.abs(A_BLOCK + i - j)
    kpos = (jnp.arange(nb)[:, None] - 1) * A_BLOCK + jnp.arange(3 * A_BLOCK)[None, :]
    valid = (dist[None] <= A_WINDOW) & ((kpos >= 0) & (kpos < L))[:, None, :]
    slopes = alibi_slopes(A_HEADS).reshape(A_KV_HEADS, G)
    s = s - slopes[:, :, None, None] * dist.astype(jnp.float32)
    s = jnp.where(valid[None, :, None, None], s, -jnp.inf)
    sk = sink.astype(jnp.float32).reshape(1, 1, A_KV_HEADS, G, 1, 1)
    m = jnp.maximum(jnp.max(s, axis=-1, keepdims=True), sk)
    p = jnp.exp(s - m)
    p = p / (jnp.sum(p, axis=-1, keepdims=True) + jnp.exp(sk - m))
    o = jnp.einsum('bnhgqk,bnkhd->bnqhgd', p.astype(v.dtype), vb)
    return o.reshape(B, L, A_HEADS * d)


def retention_dir(q, k, v, log_gamma):
    B, H, L, dk = q.shape
    dv = v.shape[-1]
    C = R_CHUNK
    nc = L // C
    qc = q.reshape(B, H, nc, C, dk)
    kc = k.reshape(B, H, nc, C, dk)
    vc = v.reshape(B, H, nc, C, dv)
    idx = jnp.arange(C, dtype=jnp.float32)
    diff = idx[:, None] - idx[None, :]
    lg = log_gamma[:, None, None]
    decay = jnp.where(diff >= 0, jnp.exp(jnp.maximum(diff, 0.0) * lg), 0.0)
    inner = jnp.einsum('bhnid,bhnjd->bhnij', qc, kc) * decay[None, :, None]
    o_inner = jnp.einsum('bhnij,bhnjv->bhniv', inner, vc)
    w_k = jnp.exp((C - 1.0 - idx)[None, :] * log_gamma[:, None])
    kv = jnp.einsum('bhncd,bhncv->nbhdv', kc * w_k[None, :, None, :, None], vc)
    g_chunk = jnp.exp(C * log_gamma)[None, :, None, None]

    def step(state, kv_n):
        return state * g_chunk + kv_n, state

    _, state_prev = lax.scan(step, jnp.zeros((B, H, dk, dv), jnp.float32), kv)
    w_q = jnp.exp((idx + 1.0)[None, :] * log_gamma[:, None])
    o_cross = jnp.einsum('bhnid,nbhdv->bhniv', qc * w_q[None, :, None, :, None], state_prev)
    return (o_inner + o_cross).reshape(B, H, L, dv)


def bidir_retention(rq, rk, rv, rg, ret_decay):
    B, L, _ = rq.shape
    def heads(t, dh):
        return t.reshape(B, L, R_HEADS, dh).transpose(0, 2, 1, 3).astype(jnp.float32)
    q = heads(rq, R_DK)
    k = heads(rk, R_DK) * (R_DK ** -0.5)
    v = heads(rv, R_DV)
    lg = jax.nn.log_sigmoid(ret_decay.astype(jnp.float32))
    o_f = retention_dir(q, k, v, lg[0])
    o_b = jnp.flip(retention_dir(jnp.flip(q, 2), jnp.flip(k, 2), jnp.flip(v, 2), lg[1]), 2)
    o = (o_f + o_b).transpose(0, 2, 1, 3)
    o = o * lax.rsqrt(jnp.mean(o * o, axis=-1, keepdims=True) + EPS)
    return jax.nn.silu(rg) * o.reshape(B, L, R_V).astype(rg.dtype)


def neighbourhood_attn(q, k, v, rpb):
    B, L, H, d = q.shape
    rows = L // GRID_W
    kh = min(NA_ROWS_MAX, rows)
    kw = NA_COLS
    nbk = GRID_W // kw
    kcw = 2 * kw
    r = jnp.arange(rows)
    rs = jnp.clip(r - kh // 2, 0, rows - kh)
    ridx = rs[:, None] + jnp.arange(kh)[None, :]
    blk = jnp.arange(nbk)
    cb = jnp.clip(blk * kw - kw // 2, 0, GRID_W - kcw)
    cidx = cb[:, None] + jnp.arange(kcw)[None, :]
    qcol = blk[:, None] * kw + jnp.arange(kw)[None, :]
    cs = jnp.clip(qcol - kw // 2, 0, GRID_W - kw)
    colvalid = (cidx[:, None, :] >= cs[:, :, None]) & (cidx[:, None, :] < cs[:, :, None] + kw)
    dr = ridx - r[:, None] + (NA_ROWS_MAX - 1)
    dc = jnp.clip(cidx[:, None, :] - qcol[:, :, None] + (NA_COLS - 1), 0, 2 * NA_COLS - 2)
    bias = rpb.astype(jnp.float32)[:, dr[:, :, None, None, None], dc[None, None]]
    bias = bias.transpose(1, 3, 0, 4, 2, 5)
    kgrid = k.reshape(B, rows, GRID_W, H, d)
    vgrid = v.reshape(B, rows, GRID_W, H, d)
    kg = kgrid[:, ridx[:, :, None, None], cidx[None, None]]
    vg = vgrid[:, ridx[:, :, None, None], cidx[None, None]]
    qg = q.reshape(B, rows, nbk, kw, H, d)
    s = jnp.einsum('brnjhd,brknmhd->brnhjkm', qg, kg).astype(jnp.float32) * (d ** -0.5)
    s = s + bias[None]
    s = jnp.where(colvalid[None, None, :, None, :, None, :], s, -jnp.inf)
    p = jax.nn.softmax(s.reshape(B, rows, nbk, H, kw, kh * kcw), axis=-1)
    p = p.reshape(B, rows, nbk, H, kw, kh, kcw)
    o = jnp.einsum('brnhjkm,brknmhd->brnjhd', p.astype(v.dtype), vg)
    return o.reshape(B, L, H * d)


def mixer(h, w_in, attn_sink, ret_decay, na_rpb, w_branch, w_mix_out):
    B, L, _ = h.shape
    offsets = np.cumsum(SPLITS)[:-1].tolist()
    aq, ak, av, rq, rk, rv, rg, nq, nk, nv, ga, gb, gc = jnp.split(h @ w_in, offsets, axis=-1)
    ya = window_gqa(aq.reshape(B, L, A_HEADS, HEAD_DIM),
                    ak.reshape(B, L, A_KV_HEADS, HEAD_DIM),
                    av.reshape(B, L, A_KV_HEADS, HEAD_DIM), attn_sink)
    yb = bidir_retention(rq, rk, rv, rg, ret_decay)
    yc = neighbourhood_attn(nq.reshape(B, L, N_HEADS, HEAD_DIM),
                            nk.reshape(B, L, N_HEADS, HEAD_DIM),
                            nv.reshape(B, L, N_HEADS, HEAD_DIM), na_rpb)
    merged = (jax.nn.sigmoid(ga) * (ya @ w_branch[0])
              + jax.nn.sigmoid(gb) * (yb @ w_branch[1])
              + jax.nn.sigmoid(gc) * (yc @ w_branch[2]))
    return merged @ w_mix_out


def memory_xattn(h, mem_n, w_xq, w_xkv, w_xo):
    B, L, _ = h.shape
    M = mem_n.shape[1]
    q = (h @ w_xq).reshape(B, L, X_HEADS, X_HEAD_DIM)
    k, v = jnp.split(mem_n @ w_xkv, 2, axis=-1)
    k = k.reshape(B, M, X_HEADS, X_HEAD_DIM)
    v = v.reshape(B, M, X_HEADS, X_HEAD_DIM)
    s = jnp.einsum('blhd,bmhd->bhlm', q, k).astype(jnp.float32) * (X_HEAD_DIM ** -0.5)
    p = jax.nn.softmax(s, axis=-1)
    o = jnp.einsum('bhlm,bmhd->blhd', p.astype(v.dtype), v).reshape(B, L, D_MODEL)
    return o @ w_xo


def swiglu(h, w_gate_up, w_down):
    g, u = jnp.split(h @ w_gate_up, 2, axis=-1)
    return (jax.nn.silu(g) * u) @ w_down


def trunk(x, mem, g_mix, w_in, attn_sink, ret_decay, na_rpb, w_branch, w_mix_out,
          g_xattn, g_mem, w_xq, w_xkv, w_xo, g_ffn, w_gate_up, w_down, g_final):
    for l in range(DEPTH):
        x = x + mixer(rms_norm(x, g_mix[l]), w_in[l], attn_sink[l], ret_decay[l],
                      na_rpb[l], w_branch[l], w_mix_out[l])
        x = x + memory_xattn(rms_norm(x, g_xattn[l]), rms_norm(mem, g_mem[l]),
                             w_xq[l], w_xkv[l], w_xo[l])
        x = x + swiglu(rms_norm(x, g_ffn[l]), w_gate_up[l], w_down[l])
    return rms_norm(x, g_final)


def setup_inputs(seed: int = 0) -> dict:
    key = jax.random.key(seed)
    ks = jax.random.split(key, 24)
    f32 = jnp.float32

    def nrm(k, shape, scale):
        return jax.random.normal(k, shape, f32) * scale

    def gain(k, shape):
        return 1.0 + 0.02 * jax.random.normal(k, shape, f32)

    hr = jnp.arange(R_HEADS, dtype=f32)
    base_decay = jnp.log(jnp.exp2(5.0 + hr) - 1.0)
    return {
        "x_prompt": nrm(ks[0], (BATCH, SEQ, D_MODEL), 1.0),
        "x_sample": nrm(ks[1], (DEC_BATCH, DEC_SEQ, D_MODEL), 1.0),
        "mem_prompt": nrm(ks[2], (BATCH, MEM_LEN, D_MODEL), 1.0),
        "mem_sample": nrm(ks[3], (DEC_BATCH, MEM_LEN, D_MODEL), 1.0),
        "g_mix": gain(ks[4], (DEPTH, D_MODEL)),
        "w_in": nrm(ks[5], (DEPTH, D_MODEL, D_IN), D_MODEL ** -0.5),
        "attn_sink": nrm(ks[6], (DEPTH, A_HEADS), 1.0),
        "ret_decay": base_decay[None, None, :] + 0.1 * jax.random.normal(ks[7], (DEPTH, 2, R_HEADS), f32),
        "na_rpb": nrm(ks[8], (DEPTH, N_HEADS, 2 * NA_ROWS_MAX - 1, 2 * NA_COLS - 1), 0.1),
        "w_branch": nrm(ks[9], (DEPTH, N_BRANCH, BRANCH_W, D_MODEL), BRANCH_W ** -0.5),
        "w_mix_out": nrm(ks[10], (DEPTH, D_MODEL, D_MODEL), D_MODEL ** -0.5),
        "g_xattn": gain(ks[11], (DEPTH, D_MODEL)),
        "g_mem": gain(ks[12], (DEPTH, D_MODEL)),
        "w_xq": nrm(ks[13], (DEPTH, D_MODEL, D_MODEL), D_MODEL ** -0.5),
        "w_xkv": nrm(ks[14], (DEPTH, D_MODEL, 2 * D_MODEL), D_MODEL ** -0.5),
        "w_xo": nrm(ks[15], (DEPTH, D_MODEL, D_MODEL), D_MODEL ** -0.5),
        "g_ffn": gain(ks[16], (DEPTH, D_MODEL)),
        "w_gate_up": nrm(ks[17], (DEPTH, D_MODEL, 2 * D_FF), D_MODEL ** -0.5),
        "w_down": nrm(ks[18], (DEPTH, D_FF, D_MODEL), D_FF ** -0.5),
        "g_final": gain(ks[19], (D_MODEL,)),
    }


def reference(x_prompt, x_sample, mem_prompt, mem_sample, g_mix, w_in, attn_sink, ret_decay,
              na_rpb, w_branch, w_mix_out, g_xattn, g_mem, w_xq, w_xkv, w_xo, g_ffn,
              w_gate_up, w_down, g_final):
    y_prompt = trunk(x_prompt, mem_prompt, g_mix, w_in, attn_sink, ret_decay, na_rpb, w_branch,
                     w_mix_out, g_xattn, g_mem, w_xq, w_xkv, w_xo, g_ffn, w_gate_up, w_down, g_final)
    y_sample = trunk(x_sample, mem_sample, g_mix, w_in, attn_sink, ret_decay, na_rpb, w_branch,
                     w_mix_out, g_xattn, g_mem, w_xq, w_xkv, w_xo, g_ffn, w_gate_up, w_down, g_final)
    return (y_prompt, y_sample)
```

```python
import functools

import numpy as np
import jax
import jax.numpy as jnp
from jax import lax
from jax.experimental import pallas as pl
from jax.experimental.pallas import tpu as pltpu

F32 = jnp.float32
BF16 = jnp.bfloat16

D_MODEL = 1024
DEPTH = 2
EPS = 1e-6
HEAD_DIM = 64
A_HEADS = 8
A_KV_HEADS = 2
A_GROUPS = A_HEADS // A_KV_HEADS
A_WINDOW = 128
A_BLOCK = 128
R_HEADS = 4
R_DK = 128
R_DV = 128
R_CHUNK = 128
N_HEADS = 8
GRID_W = 64
NA_ROWS = 8
NA_COLS = 16
MEM_LEN = 256
X_HEADS = 4
X_HEAD_DIM = D_MODEL // X_HEADS
D_FF = 2816
A_Q = A_HEADS * HEAD_DIM
A_KV = A_KV_HEADS * HEAD_DIM
R_W = R_HEADS * R_DK
N_W = N_HEADS * HEAD_DIM
W_A = A_Q + 2 * A_KV
W_R = 4 * R_W
W_N = 3 * N_W
W_G = 3 * D_MODEL
D_IN = W_A + W_R + W_N + W_G
NEG = -1e30

LANE = 128
MXU_N = 256
VMEM_LIMIT = 56 * 1024 * 1024


def _params(n_axes):
    return pltpu.CompilerParams(
        dimension_semantics=("arbitrary",) * n_axes, vmem_limit_bytes=VMEM_LIMIT)


def _resident(shape):
    return pl.BlockSpec(shape, lambda *_: (0,) * len(shape), pipeline_mode=pl.Buffered(1))


def _rms(x, g):
    return x * lax.rsqrt(jnp.mean(x * x, axis=-1, keepdims=True) + EPS) * g


def _sigmoid(x):
    return 1.0 / (1.0 + jnp.exp(-x))


def _dot(a, b):
    return jnp.dot(a, b, preferred_element_type=F32)


def _dot_nt(a, b):
    return lax.dot_general(a, b, (((1,), (1,)), ((), ())), preferred_element_type=F32)


def _dot_tn(a, b):
    return lax.dot_general(a, b, (((0,), (0,)), ((), ())), preferred_element_type=F32)


def _proj_kernel(x_ref, g_ref, w_ref, oa_ref, or_ref, on_ref, og_ref):
    h = _rms(x_ref[0], g_ref[...]).astype(BF16)
    col = 0
    for o_ref, width in ((oa_ref, W_A), (or_ref, W_R), (on_ref, W_N), (og_ref, W_G)):
        for c in range(0, width, 2 * MXU_N):
            n = min(2 * MXU_N, width - c)
            o_ref[0, :, c:c + n] = _dot(h, w_ref[:, col + c:col + c + n]).astype(o_ref.dtype)
        col += width


def _proj(x, g, w, tm):
    B, L, _ = x.shape
    row = lambda b, i: (b, i, 0)
    return pl.pallas_call(
        _proj_kernel,
        grid=(B, L // tm),
        in_specs=[pl.BlockSpec((1, tm, D_MODEL), row), _resident((1, D_MODEL)),
                  _resident((D_MODEL, D_IN))],
        out_specs=[pl.BlockSpec((1, tm, W_A), row), pl.BlockSpec((1, tm, W_R), row),
                   pl.BlockSpec((1, tm, W_N), row), pl.BlockSpec((1, tm, W_G), row)],
        out_shape=[jax.ShapeDtypeStruct((B, L, W_A), BF16), jax.ShapeDtypeStruct((B, L, W_R), F32),
                   jax.ShapeDtypeStruct((B, L, W_N), BF16), jax.ShapeDtypeStruct((B, L, W_G), F32)],
        compiler_params=_params(2), name="proj_in",
    )(x, g, w)


def _wattn_kernel(q_ref, k0_ref, k1_ref, k2_ref, v0_ref, v1_ref, v2_ref, bias_ref, sink_ref,
                  o_ref, *, seq_len):
    n = pl.program_id(1)
    q = q_ref[0]
    k = jnp.concatenate([k0_ref[0], k1_ref[0], k2_ref[0]], axis=0)
    v = jnp.concatenate([v0_ref[0], v1_ref[0], v2_ref[0]], axis=0)
    kpos = (n - 1) * A_BLOCK + lax.broadcasted_iota(jnp.int32, (1, 3 * A_BLOCK), 1)
    kvalid = (kpos >= 0) & (kpos < seq_len)
    outs = []
    for h in range(A_KV_HEADS):
        kh = k[:, h * HEAD_DIM:(h + 1) * HEAD_DIM]
        vh = v[:, h * HEAD_DIM:(h + 1) * HEAD_DIM]
        qg = jnp.concatenate(
            [q[:, (h * A_GROUPS + g) * HEAD_DIM:(h * A_GROUPS + g + 1) * HEAD_DIM]
             for g in range(A_GROUPS)], axis=0)
        s = _dot_nt(qg, kh) * (HEAD_DIM ** -0.5) + bias_ref[h]
        s = jnp.where(kvalid, s, NEG)
        sk = sink_ref[h]
        m = jnp.maximum(jnp.max(s, axis=-1, keepdims=True), sk)
        p = jnp.exp(s - m)
        den = jnp.sum(p, axis=-1, keepdims=True) + jnp.exp(sk - m)
        o = _dot(p.astype(BF16), vh) / den
        outs += [o[g * A_BLOCK:(g + 1) * A_BLOCK] for g in range(A_GROUPS)]
    o_ref[0] = jnp.concatenate(outs, axis=1).astype(o_ref.dtype)


def _wattn(pa, bias, sink):
    B, L, _ = pa.shape
    nb = L // A_BLOCK
    kcol, vcol = A_Q // A_KV, A_Q // A_KV + 1
    prev = lambda c: (lambda b, n: (b, jnp.maximum(n - 1, 0), c))
    cur = lambda c: (lambda b, n: (b, n, c))
    nxt = lambda c: (lambda b, n: (b, jnp.minimum(n + 1, nb - 1), c))
    kv = lambda f, c: pl.BlockSpec((1, A_BLOCK, A_KV), f(c))
    return pl.pallas_call(
        functools.partial(_wattn_kernel, seq_len=L),
        grid=(B, nb),
        in_specs=[pl.BlockSpec((1, A_BLOCK, A_Q), cur(0)),
                  kv(prev, kcol), kv(cur, kcol), kv(nxt, kcol),
                  kv(prev, vcol), kv(cur, vcol), kv(nxt, vcol),
                  _resident(bias.shape), _resident(sink.shape)],
        out_specs=pl.BlockSpec((1, A_BLOCK, A_Q), cur(0)),
        out_shape=jax.ShapeDtypeStruct((B, L, A_Q), BF16),
        compiler_params=_params(2), name="window_gqa",
    )(pa, pa, pa, pa, pa, pa, pa, bias, sink)


def _wattn_tables(attn_sink):
    i = np.arange(A_BLOCK)[:, None]
    j = np.arange(3 * A_BLOCK)[None, :]
    dist = np.abs(A_BLOCK + i - j)
    slopes = jnp.exp2(-8.0 * jnp.arange(1, A_HEADS + 1, dtype=F32) / A_HEADS)
    bias = -slopes[:, None, None] * jnp.asarray(dist, F32)[None]
    bias = jnp.where(jnp.asarray(dist <= A_WINDOW)[None], bias, NEG)
    bias = bias.reshape(A_KV_HEADS, A_GROUPS * A_BLOCK, 3 * A_BLOCK)
    sink = jnp.repeat(attn_sink.astype(F32), A_BLOCK).reshape(A_KV_HEADS, A_GROUPS * A_BLOCK, 1)
    return bias, sink


def _ret_kernel(lg_ref, gc_ref, q_ref, k_ref, v_ref, g_ref, o_ref, sb_ref, *, nc):
    h = pl.program_id(1)
    C = R_CHUNK
    lgf, lgb = lg_ref[0, h], lg_ref[1, h]
    gcf, gcb = gc_ref[0, h], gc_ref[1, h]
    diff = (lax.broadcasted_iota(jnp.int32, (C, C), 0)
            - lax.broadcasted_iota(jnp.int32, (C, C), 1)).astype(F32)
    decay = (jnp.where(diff >= 0, jnp.exp(jnp.maximum(diff, 0.0) * lgf), 0.0)
             + jnp.where(diff <= 0, jnp.exp(jnp.maximum(-diff, 0.0) * lgb), 0.0))
    idx = lax.broadcasted_iota(jnp.int32, (C, 1), 0).astype(F32)
    wq_f = jnp.exp((idx + 1.0) * lgf)
    wk_f = jnp.exp((C - 1.0 - idx) * lgf)
    wq_b = jnp.exp((C - idx) * lgb)
    wk_b = jnp.exp(idx * lgb)

    def rows(n):
        return pl.ds(pl.multiple_of(n * C, C), C)

    def kv_of(n, wk):
        k = k_ref[0, rows(n), :] * (R_DK ** -0.5)
        return _dot_tn((k * wk).astype(BF16), v_ref[0, rows(n), :].astype(BF16))

    def bwd(t, s):
        n = nc - 1 - t
        sb_ref[n] = s
        return s * gcb + kv_of(n, wk_b)

    lax.fori_loop(0, nc, bwd, jnp.zeros((R_DK, R_DV), F32))

    def fwd(n, s):
        q = q_ref[0, rows(n), :]
        k = k_ref[0, rows(n), :] * (R_DK ** -0.5)
        v = v_ref[0, rows(n), :].astype(BF16)
        inner = _dot_nt(q.astype(BF16), k.astype(BF16)) * decay
        o = _dot(inner.astype(BF16), v)
        o += _dot((q * wq_f).astype(BF16), s.astype(BF16))
        o += _dot((q * wq_b).astype(BF16), sb_ref[n].astype(BF16))
        o = o * lax.rsqrt(jnp.mean(o * o, axis=-1, keepdims=True) + EPS)
        g = g_ref[0, rows(n), :]
        o_ref[0, rows(n), :] = (g * _sigmoid(g) * o).astype(o_ref.dtype)
        return s * gcf + _dot_tn((k * wk_f).astype(BF16), v)

    lax.fori_loop(0, nc, fwd, jnp.zeros((R_DK, R_DV), F32))


def _retention(pr, lg, gchunk):
    B, L, _ = pr.shape
    nc = L // R_CHUNK
    part = lambda p: pl.BlockSpec((1, L, R_DK), lambda b, h: (b, 0, p * R_HEADS + h))
    smem = pl.BlockSpec(memory_space=pltpu.SMEM)
    return pl.pallas_call(
        functools.partial(_ret_kernel, nc=nc),
        grid=(B, R_HEADS),
        in_specs=[smem, smem, part(0), part(1), part(2), part(3)],
        out_specs=pl.BlockSpec((1, L, R_DV), lambda b, h: (b, 0, h)),
        out_shape=jax.ShapeDtypeStruct((B, L, R_W), BF16),
        scratch_shapes=[pltpu.VMEM((nc, R_DK, R_DV), F32)],
        compiler_params=_params(2), name="retention",
    )(lg, gchunk, pr, pr, pr, pr)


def _na_kernel(q_ref, k_ref, v_ref, bias_ref, o_ref, *, n_rows):
    keys = NA_ROWS * GRID_W

    def body(r, carry):
        rs = jnp.clip(r - NA_ROWS // 2, 0, n_rows - NA_ROWS)
        cls = r - rs
        qrows = pl.ds(pl.multiple_of(r * GRID_W, GRID_W), GRID_W)
        krows = pl.ds(pl.multiple_of(rs * GRID_W, GRID_W), keys)
        q = q_ref[0, qrows, :]
        k = k_ref[0, krows, :]
        v = v_ref[0, krows, :]
        outs = []
        for h in range(2):
            sl = slice(h * HEAD_DIM, (h + 1) * HEAD_DIM)
            s = _dot_nt(q[:, sl], k[:, sl]) * (HEAD_DIM ** -0.5) + bias_ref[cls, h]
            m = jnp.max(s, axis=-1, keepdims=True)
            p = jnp.exp(s - m)
            den = jnp.sum(p, axis=-1, keepdims=True)
            outs.append(_dot(p.astype(BF16), v[:, sl]) / den)
        o_ref[0, qrows, :] = jnp.concatenate(outs, axis=1).astype(o_ref.dtype)
        return carry

    lax.fori_loop(0, n_rows, body, 0)


def _nattn(pn, bias):
    B, L, _ = pn.shape
    pairs = N_HEADS // 2
    part = lambda p: pl.BlockSpec((1, L, 2 * HEAD_DIM), lambda b, hp: (b, 0, p * pairs + hp))
    return pl.pallas_call(
        functools.partial(_na_kernel, n_rows=L // GRID_W),
        grid=(B, pairs),
        in_specs=[part(0), part(1), part(2),
                  pl.BlockSpec((NA_ROWS, 2, GRID_W, NA_ROWS * GRID_W), lambda b, hp: (0, hp, 0, 0))],
        out_specs=pl.BlockSpec((1, L, 2 * HEAD_DIM), lambda b, hp: (b, 0, hp)),
        out_shape=jax.ShapeDtypeStruct((B, L, N_W), BF16),
        compiler_params=_params(2), name="neighbourhood_attn",
    )(pn, pn, pn, bias)


def _na_bias(rpb):
    c = np.arange(GRID_W)
    cs = np.clip(c - NA_COLS // 2, 0, GRID_W - NA_COLS)
    valid = (c[None, :] >= cs[:, None]) & (c[None, :] < cs[:, None] + NA_COLS)
    dc = np.clip(c[None, :] - c[:, None] + NA_COLS - 1, 0, 2 * NA_COLS - 2)
    dr = np.arange(NA_ROWS)[None, :] - np.arange(NA_ROWS)[:, None] + NA_ROWS - 1
    b = rpb.astype(F32)[:, dr[:, :, None, None], dc[None, None]]
    b = jnp.where(jnp.asarray(valid)[None, None, None], b, NEG)
    return b.transpose(1, 0, 3, 2, 4).reshape(NA_ROWS, N_HEADS, GRID_W, NA_ROWS * GRID_W)


def _merge_kernel(x_ref, ya_ref, yb_ref, yc_ref, gate_ref, wb_ref, wo_ref, o_ref):
    merged = None
    for i, y_ref in enumerate((ya_ref, yb_ref, yc_ref)):
        gate = _sigmoid(gate_ref[0, :, i * D_MODEL:(i + 1) * D_MODEL])
        t = gate * _dot(y_ref[0], wb_ref[i])
        merged = t if merged is None else merged + t
    o_ref[0] = x_ref[0] + _dot(merged.astype(BF16), wo_ref[...])


def _merge(x, ya, yb, yc, gates, wb, wo, tm):
    B, L, _ = x.shape
    row = lambda b, i: (b, i, 0)
    y_spec = pl.BlockSpec((1, tm, 512), row)
    return pl.pallas_call(
        _merge_kernel,
        grid=(B, L // tm),
        in_specs=[pl.BlockSpec((1, tm, D_MODEL), row), y_spec, y_spec, y_spec,
                  pl.BlockSpec((1, tm, W_G), row), _resident(wb.shape), _resident(wo.shape)],
        out_specs=pl.BlockSpec((1, tm, D_MODEL), row),
        out_shape=jax.ShapeDtypeStruct(x.shape, F32),
        compiler_params=_params(2), name="merge_out",
    )(x, ya, yb, yc, gates, wb, wo)


def _memkv_kernel(m_ref, g_ref, w_ref, o_ref):
    h = _rms(m_ref[0], g_ref[...]).astype(BF16)
    for c in range(0, 2 * D_MODEL, 2 * MXU_N):
        o_ref[0, :, c:c + 2 * MXU_N] = _dot(h, w_ref[:, c:c + 2 * MXU_N]).astype(o_ref.dtype)


def _memkv(mem, g, w):
    B, M, _ = mem.shape
    return pl.pallas_call(
        _memkv_kernel,
        grid=(B,),
        in_specs=[pl.BlockSpec((1, M, D_MODEL), lambda b: (b, 0, 0)), _resident((1, D_MODEL)),
                  _resident(w.shape)],
        out_specs=pl.BlockSpec((1, M, 2 * D_MODEL), lambda b: (b, 0, 0)),
        out_shape=jax.ShapeDtypeStruct((B, M, 2 * D_MODEL), BF16),
        compiler_params=_params(1), name="mem_kv",
    )(mem, g, w)


def _xattn_kernel(x_ref, g_ref, wq_ref, kv_ref, wo_ref, o_ref):
    x = x_ref[0]
    q = _dot(_rms(x, g_ref[...]).astype(BF16), wq_ref[...]).astype(BF16)
    outs = []
    for h in range(X_HEADS):
        lo = h * X_HEAD_DIM
        kh = kv_ref[0, :, lo:lo + X_HEAD_DIM]
        vh = kv_ref[0, :, D_MODEL + lo:D_MODEL + lo + X_HEAD_DIM]
        s = _dot_nt(q[:, lo:lo + X_HEAD_DIM], kh) * (X_HEAD_DIM ** -0.5)
        p = jnp.exp(s - jnp.max(s, axis=-1, keepdims=True))
        den = jnp.sum(p, axis=-1, keepdims=True)
        outs.append((_dot(p.astype(BF16), vh) / den).astype(BF16))
    o_ref[0] = x + _dot(jnp.concatenate(outs, axis=1), wo_ref[...])


def _xattn(x, g, wq, kv, wo, tm):
    B, L, _ = x.shape
    row = lambda b, i: (b, i, 0)
    return pl.pallas_call(
        _xattn_kernel,
        grid=(B, L // tm),
        in_specs=[pl.BlockSpec((1, tm, D_MODEL), row), _resident((1, D_MODEL)), _resident(wq.shape),
                  pl.BlockSpec((1, MEM_LEN, 2 * D_MODEL), lambda b, i: (b, 0, 0)),
                  _resident(wo.shape)],
        out_specs=pl.BlockSpec((1, tm, D_MODEL), row),
        out_shape=jax.ShapeDtypeStruct(x.shape, F32),
        compiler_params=_params(2), name="mem_xattn",
    )(x, g, wq, kv, wo)


def _ffn_kernel(x_ref, g_ref, wgu_ref, wd_ref, gf_ref, o_ref, act_ref, *, final_norm):
    x = x_ref[0]
    h = _rms(x, g_ref[...]).astype(BF16)
    for c in range(0, D_FF, MXU_N):
        gate = _dot(h, wgu_ref[:, c:c + MXU_N])
        up = _dot(h, wgu_ref[:, D_FF + c:D_FF + c + MXU_N])
        act_ref[:, c:c + MXU_N] = (gate * _sigmoid(gate) * up).astype(BF16)
    y = x + _dot(act_ref[...], wd_ref[...])
    if final_norm:
        y = _rms(y, gf_ref[...])
    o_ref[0] = y


def _ffn(x, g, wgu, wd, g_final, final_norm, tm):
    B, L, _ = x.shape
    row = lambda b, i: (b, i, 0)
    return pl.pallas_call(
        functools.partial(_ffn_kernel, final_norm=final_norm),
        grid=(B, L // tm),
        in_specs=[pl.BlockSpec((1, tm, D_MODEL), row), _resident((1, D_MODEL)),
                  _resident(wgu.shape), _resident(wd.shape), _resident((1, D_MODEL))],
        out_specs=pl.BlockSpec((1, tm, D_MODEL), row),
        out_shape=jax.ShapeDtypeStruct(x.shape, F32),
        scratch_shapes=[pltpu.VMEM((tm, D_FF), BF16)],
        compiler_params=_params(2), name="swiglu",
    )(x, g, wgu, wd, g_final)


TM = 256


def _trunk(x, mem, layers, g_final):
    for l, p in enumerate(layers):
        pa, pr, pn, gates = _proj(x, p["g_mix"], p["w_in"], TM)
        ya = _wattn(pa, p["a_bias"], p["a_sink"])
        yb = _retention(pr, p["r_lg"], p["r_gchunk"])
        yc = _nattn(pn, p["n_bias"])
        x = _merge(x, ya, yb, yc, gates, p["w_branch"], p["w_mix_out"], TM)
        kv = _memkv(mem, p["g_mem"], p["w_xkv"])
        x = _xattn(x, p["g_xattn"], p["w_xq"], kv, p["w_xo"], TM)
        x = _ffn(x, p["g_ffn"], p["w_gate_up"], p["w_down"], g_final, l == DEPTH - 1, TM)
    return x


def kernel(x_prompt, x_sample, mem_prompt, mem_sample, g_mix, w_in, attn_sink, ret_decay, na_rpb,
           w_branch, w_mix_out, g_xattn, g_mem, w_xq, w_xkv, w_xo, g_ffn, w_gate_up, w_down, g_final):
    layers = []
    for l in range(DEPTH):
        a_bias, a_sink = _wattn_tables(attn_sink[l])
        lg = jax.nn.log_sigmoid(ret_decay[l].astype(F32))
        layers.append(dict(
            g_mix=g_mix[l].reshape(1, D_MODEL), w_in=w_in[l].astype(BF16),
            a_bias=a_bias, a_sink=a_sink,
            r_lg=lg, r_gchunk=jnp.exp(R_CHUNK * lg),
            n_bias=_na_bias(na_rpb[l]),
            w_branch=w_branch[l].astype(BF16), w_mix_out=w_mix_out[l].astype(BF16),
            g_xattn=g_xattn[l].reshape(1, D_MODEL), g_mem=g_mem[l].reshape(1, D_MODEL),
            w_xq=w_xq[l].astype(BF16), w_xkv=w_xkv[l].astype(BF16), w_xo=w_xo[l].astype(BF16),
            g_ffn=g_ffn[l].reshape(1, D_MODEL),
            w_gate_up=w_gate_up[l].astype(BF16), w_down=w_down[l].astype(BF16)))
    gf = g_final.reshape(1, D_MODEL)
    return (_trunk(x_prompt, mem_prompt, layers, gf), _trunk(x_sample, mem_sample, layers, gf))
```

```python
import functools

import numpy as np
import jax
import jax.numpy as jnp
from jax import lax
from jax.experimental import pallas as pl
from jax.experimental.pallas import tpu as pltpu

F32 = jnp.float32
BF16 = jnp.bfloat16

D_MODEL = 1024
DEPTH = 2
EPS = 1e-6
HEAD_DIM = 64
A_HEADS = 8
A_KV_HEADS = 2
A_GROUPS = A_HEADS // A_KV_HEADS
A_WINDOW = 128
A_BLOCK = 128
R_HEADS = 4
R_DK = 128
R_DV = 128
R_CHUNK = 128
N_HEADS = 8
GRID_W = 64
NA_ROWS = 8
NA_COLS = 16
MEM_LEN = 256
X_HEADS = 4
X_HEAD_DIM = D_MODEL // X_HEADS
D_FF = 2816
A_Q = A_HEADS * HEAD_DIM
A_KV = A_KV_HEADS * HEAD_DIM
R_W = R_HEADS * R_DK
N_W = N_HEADS * HEAD_DIM
W_A = A_Q + 2 * A_KV
W_R = 4 * R_W
W_N = 3 * N_W
W_G = 3 * D_MODEL
D_IN = W_A + W_R + W_N + W_G
NEG = -1e30

LANE = 128
MXU_N = 256
VMEM_LIMIT = 56 * 1024 * 1024


def _params(n_axes):
    return pltpu.CompilerParams(
        dimension_semantics=("arbitrary",) * n_axes, vmem_limit_bytes=VMEM_LIMIT)


def _resident(shape):
    return pl.BlockSpec(shape, lambda *_: (0,) * len(shape), pipeline_mode=pl.Buffered(1))


def _rms(x, g):
    return x * lax.rsqrt(jnp.mean(x * x, axis=-1, keepdims=True) + EPS) * g


def _sigmoid(x):
    return 1.0 / (1.0 + jnp.exp(-x))


def _dot(a, b):
    return jnp.dot(a, b, preferred_element_type=F32)


def _dot_nt(a, b):
    return lax.dot_general(a, b, (((1,), (1,)), ((), ())), preferred_element_type=F32)


def _dot_tn(a, b):
    return lax.dot_general(a, b, (((0,), (0,)), ((), ())), preferred_element_type=F32)


def _proj_kernel(x_ref, g_ref, w_ref, oa_ref, or_ref, on_ref, og_ref):
    h = _rms(x_ref[0], g_ref[...]).astype(BF16)
    col = 0
    for o_ref, width in ((oa_ref, W_A), (or_ref, W_R), (on_ref, W_N), (og_ref, W_G)):
        for c in range(0, width, 2 * MXU_N):
            n = min(2 * MXU_N, width - c)
            o_ref[0, :, c:c + n] = _dot(h, w_ref[:, col + c:col + c + n]).astype(o_ref.dtype)
        col += width


def _proj(x, g, w, tm):
    B, L, _ = x.shape
    row = lambda b, i: (b, i, 0)
    return pl.pallas_call(
        _proj_kernel,
        grid=(B, L // tm),
        in_specs=[pl.BlockSpec((1, tm, D_MODEL), row), _resident((1, D_MODEL)),
                  _resident((D_MODEL, D_IN))],
        out_specs=[pl.BlockSpec((1, tm, W_A), row), pl.BlockSpec((1, tm, W_R), row),
                   pl.BlockSpec((1, tm, W_N), row), pl.BlockSpec((1, tm, W_G), row)],
        out_shape=[jax.ShapeDtypeStruct((B, L, W_A), BF16), jax.ShapeDtypeStruct((B, L, W_R), F32),
                   jax.ShapeDtypeStruct((B, L, W_N), BF16), jax.ShapeDtypeStruct((B, L, W_G), F32)],
        compiler_params=_params(2), name="proj_in",
    )(x, g, w)


def _wattn_kernel(q_ref, k0_ref, k1_ref, k2_ref, v0_ref, v1_ref, v2_ref, bias_ref, sink_ref,
                  o_ref, *, seq_len):
    n = pl.program_id(1)
    q = q_ref[0]
    k = jnp.concatenate([k0_ref[0], k1_ref[0], k2_ref[0]], axis=0)
    v = jnp.concatenate([v0_ref[0], v1_ref[0], v2_ref[0]], axis=0)
    kpos = (n - 1) * A_BLOCK + lax.broadcasted_iota(jnp.int32, (1, 3 * A_BLOCK), 1)
    kvalid = (kpos >= 0) & (kpos < seq_len)
    outs = []
    for h in range(A_KV_HEADS):
        kh = k[:, h * HEAD_DIM:(h + 1) * HEAD_DIM]
        vh = v[:, h * HEAD_DIM:(h + 1) * HEAD_DIM]
        qg = jnp.concatenate(
            [q[:, (h * A_GROUPS + g) * HEAD_DIM:(h * A_GROUPS + g + 1) * HEAD_DIM]
             for g in range(A_GROUPS)], axis=0)
        s = _dot_nt(qg, kh) * (HEAD_DIM ** -0.5) + bias_ref[h]
        s = jnp.where(kvalid, s, NEG)
        sk = sink_ref[h]
        m = jnp.maximum(jnp.max(s, axis=-1, keepdims=True), sk)
        p = jnp.exp(s - m)
        den = jnp.sum(p, axis=-1, keepdims=True) + jnp.exp(sk - m)
        o = _dot(p.astype(BF16), vh) / den
        outs += [o[g * A_BLOCK:(g + 1) * A_BLOCK] for g in range(A_GROUPS)]
    o_ref[0] = jnp.concatenate(outs, axis=1).astype(o_ref.dtype)


def _wattn(pa, bias, sink):
    B, L, _ = pa.shape
    nb = L // A_BLOCK
    kcol, vcol = A_Q // A_KV, A_Q // A_KV + 1
    prev = lambda c: (lambda b, n: (b, jnp.maximum(n - 1, 0), c))
    cur = lambda c: (lambda b, n: (b, n, c))
    nxt = lambda c: (lambda b, n: (b, jnp.minimum(n + 1, nb - 1), c))
    kv = lambda f, c: pl.BlockSpec((1, A_BLOCK, A_KV), f(c))
    return pl.pallas_call(
        functools.partial(_wattn_kernel, seq_len=L),
        grid=(B, nb),
        in_specs=[pl.BlockSpec((1, A_BLOCK, A_Q), cur(0)),
                  kv(prev, kcol), kv(cur, kcol), kv(nxt, kcol),
                  kv(prev, vcol), kv(cur, vcol), kv(nxt, vcol),
                  _resident(bias.shape), _resident(sink.shape)],
        out_specs=pl.BlockSpec((1, A_BLOCK, A_Q), cur(0)),
        out_shape=jax.ShapeDtypeStruct((B, L, A_Q), BF16),
        compiler_params=_params(2), name="window_gqa",
    )(pa, pa, pa, pa, pa, pa, pa, bias, sink)


def _wattn_tables(attn_sink):
    i = np.arange(A_BLOCK)[:, None]
    j = np.arange(3 * A_BLOCK)[None, :]
    dist = np.abs(A_BLOCK + i - j)
    slopes = jnp.exp2(-8.0 * jnp.arange(1, A_HEADS + 1, dtype=F32) / A_HEADS)
    bias = -slopes[:, None, None] * jnp.asarray(dist, F32)[None]
    bias = jnp.where(jnp.asarray(dist <= A_WINDOW)[None], bias, NEG)
    bias = bias.reshape(A_KV_HEADS, A_GROUPS * A_BLOCK, 3 * A_BLOCK)
    sink = jnp.repeat(attn_sink.astype(F32), A_BLOCK).reshape(A_KV_HEADS, A_GROUPS * A_BLOCK, 1)
    return bias, sink


def _ret_kernel(lg_ref, gc_ref, q_ref, k_ref, v_ref, g_ref, o_ref, sb_ref, *, nc):
    h = pl.program_id(1)
    C = R_CHUNK
    lgf, lgb = lg_ref[0, h], lg_ref[1, h]
    gcf, gcb = gc_ref[0, h], gc_ref[1, h]
    diff = (lax.broadcasted_iota(jnp.int32, (C, C), 0)
            - lax.broadcasted_iota(jnp.int32, (C, C), 1)).astype(F32)
    decay = (jnp.where(diff >= 0, jnp.exp(jnp.maximum(diff, 0.0) * lgf), 0.0)
             + jnp.where(diff <= 0, jnp.exp(jnp.maximum(-diff, 0.0) * lgb), 0.0))
    idx = lax.broadcasted_iota(jnp.int32, (C, 1), 0).astype(F32)
    wq_f = jnp.exp((idx + 1.0) * lgf)
    wk_f = jnp.exp((C - 1.0 - idx) * lgf)
    wq_b = jnp.exp((C - idx) * lgb)
    wk_b = jnp.exp(idx * lgb)

    def rows(n):
        return pl.ds(pl.multiple_of(n * C, C), C)

    def kv_of(n, wk):
        k = k_ref[0, rows(n), :] * (R_DK ** -0.5)
        return _dot_tn((k * wk).astype(BF16), v_ref[0, rows(n), :].astype(BF16))

    def bwd(t, s):
        n = nc - 1 - t
        sb_ref[n] = s
        return s * gcb + kv_of(n, wk_b)

    lax.fori_loop(0, nc, bwd, jnp.zeros((R_DK, R_DV), F32))

    def fwd(n, s):
        q = q_ref[0, rows(n), :]
        k = k_ref[0, rows(n), :] * (R_DK ** -0.5)
        v = v_ref[0, rows(n), :].astype(BF16)
        inner = _dot_nt(q.astype(BF16), k.astype(BF16)) * decay
        o = _dot(inner.astype(BF16), v)
        o += _dot((q * wq_f).astype(BF16), s.astype(BF16))
        o += _dot((q * wq_b).astype(BF16), sb_ref[n].astype(BF16))
        o = o * lax.rsqrt(jnp.mean(o * o, axis=-1, keepdims=True) + EPS)
        g = g_ref[0, rows(n), :]
        o_ref[0, rows(n), :] = (g * _sigmoid(g) * o).astype(o_ref.dtype)
        return s * gcf + _dot_tn((k * wk_f).astype(BF16), v)

    lax.fori_loop(0, nc, fwd, jnp.zeros((R_DK, R_DV), F32))


def _retention(pr, lg, gchunk):
    B, L, _ = pr.shape
    nc = L // R_CHUNK
    part = lambda p: pl.BlockSpec((1, L, R_DK), lambda b, h: (b, 0, p * R_HEADS + h))
    smem = pl.BlockSpec(memory_space=pltpu.SMEM)
    return pl.pallas_call(
        functools.partial(_ret_kernel, nc=nc),
        grid=(B, R_HEADS),
        in_specs=[smem, smem, part(0), part(1), part(2), part(3)],
        out_specs=pl.BlockSpec((1, L, R_DV), lambda b, h: (b, 0, h)),
        out_shape=jax.ShapeDtypeStruct((B, L, R_W), BF16),
        scratch_shapes=[pltpu.VMEM((nc, R_DK, R_DV), F32)],
        compiler_params=_params(2), name="retention",
    )(lg, gchunk, pr, pr, pr, pr)


NA_UNROLL = 16


def _na_kernel(q_ref, k_ref, v_ref, bias_ref, o_ref, *, n_rows):
    keys = NA_ROWS * GRID_W
    head0 = lax.broadcasted_iota(jnp.int32, (GRID_W, 2 * HEAD_DIM), 1) < HEAD_DIM

    def one_row(r):
        rs = jnp.clip(r - NA_ROWS // 2, 0, n_rows - NA_ROWS)
        qrows = pl.ds(pl.multiple_of(r * GRID_W, GRID_W), GRID_W)
        krows = pl.ds(pl.multiple_of(rs * GRID_W, GRID_W), keys)
        q = q_ref[0, qrows, :].astype(F32)
        q2 = jnp.concatenate([jnp.where(head0, q, 0.0), jnp.where(head0, 0.0, q)], axis=0).astype(BF16)
        s = _dot_nt(k_ref[0, krows, :], q2) * (HEAD_DIM ** -0.5)
        s = s + jnp.concatenate(
            [bias_ref[0, rs - r + (NA_ROWS - 1) + j] for j in range(NA_ROWS)], axis=0)
        p = jnp.exp(s - jnp.max(s, axis=0, keepdims=True))
        p = p * (1.0 / jnp.sum(p, axis=0, keepdims=True))
        o = _dot_tn(p.astype(BF16), v_ref[0, krows, :])
        o_ref[0, qrows, :] = jnp.where(head0, o[:GRID_W], o[GRID_W:]).astype(o_ref.dtype)

    def body(i, carry):
        for u in range(NA_UNROLL):
            one_row(i * NA_UNROLL + u)
        return carry

    lax.fori_loop(0, n_rows // NA_UNROLL, body, 0)


def _nattn(pn, bias):
    B, L, _ = pn.shape
    pairs = N_HEADS // 2
    assert (L // GRID_W) % NA_UNROLL == 0
    part = lambda p: pl.BlockSpec((1, L, 2 * HEAD_DIM), lambda b, hp: (b, 0, p * pairs + hp))
    return pl.pallas_call(
        functools.partial(_na_kernel, n_rows=L // GRID_W),
        grid=(B, pairs),
        in_specs=[part(0), part(1), part(2),
                  pl.BlockSpec((1,) + bias.shape[1:], lambda b, hp: (hp, 0, 0, 0))],
        out_specs=pl.BlockSpec((1, L, 2 * HEAD_DIM), lambda b, hp: (b, 0, hp)),
        out_shape=jax.ShapeDtypeStruct((B, L, N_W), BF16),
        compiler_params=_params(2), name="neighbourhood_attn",
    )(pn, pn, pn, bias)


def _na_bias(rpb):
    c = np.arange(GRID_W)
    cs = np.clip(c - NA_COLS // 2, 0, GRID_W - NA_COLS)
    valid = (c[:, None] >= cs[None, :]) & (c[:, None] < cs[None, :] + NA_COLS)
    dc = c[:, None] - c[None, :] + NA_COLS - 1
    onehot = valid[None] & (dc[None] == np.arange(2 * NA_COLS - 1)[:, None, None])
    t = jnp.where(jnp.asarray(onehot)[None, None], rpb.astype(F32)[:, :, :, None, None], 0.0).sum(axis=2)
    t = jnp.where(jnp.asarray(valid)[None, None], t, NEG)
    t = t.reshape(N_HEADS // 2, 2, 2 * NA_ROWS - 1, GRID_W, GRID_W)
    return t.transpose(0, 2, 3, 1, 4).reshape(N_HEADS // 2, 2 * NA_ROWS - 1, GRID_W, 2 * GRID_W)


def _merge_kernel(x_ref, ya_ref, yb_ref, yc_ref, gate_ref, wb_ref, wo_ref, o_ref):
    merged = None
    for i, y_ref in enumerate((ya_ref, yb_ref, yc_ref)):
        gate = _sigmoid(gate_ref[0, :, i * D_MODEL:(i + 1) * D_MODEL])
        t = gate * _dot(y_ref[0], wb_ref[i])
        merged = t if merged is None else merged + t
    o_ref[0] = x_ref[0] + _dot(merged.astype(BF16), wo_ref[...])


def _merge(x, ya, yb, yc, gates, wb, wo, tm):
    B, L, _ = x.shape
    row = lambda b, i: (b, i, 0)
    y_spec = pl.BlockSpec((1, tm, 512), row)
    return pl.pallas_call(
        _merge_kernel,
        grid=(B, L // tm),
        in_specs=[pl.BlockSpec((1, tm, D_MODEL), row), y_spec, y_spec, y_spec,
                  pl.BlockSpec((1, tm, W_G), row), _resident(wb.shape), _resident(wo.shape)],
        out_specs=pl.BlockSpec((1, tm, D_MODEL), row),
        out_shape=jax.ShapeDtypeStruct(x.shape, F32),
        compiler_params=_params(2), name="merge_out",
    )(x, ya, yb, yc, gates, wb, wo)


def _memkv_kernel(m_ref, g_ref, w_ref, o_ref):
    h = _rms(m_ref[0], g_ref[...]).astype(BF16)
    for c in range(0, 2 * D_MODEL, 2 * MXU_N):
        o_ref[0, :, c:c + 2 * MXU_N] = _dot(h, w_ref[:, c:c + 2 * MXU_N]).astype(o_ref.dtype)


def _memkv(mem, g, w):
    B, M, _ = mem.shape
    return pl.pallas_call(
        _memkv_kernel,
        grid=(B,),
        in_specs=[pl.BlockSpec((1, M, D_MODEL), lambda b: (b, 0, 0)), _resident((1, D_MODEL)),
                  _resident(w.shape)],
        out_specs=pl.BlockSpec((1, M, 2 * D_MODEL), lambda b: (b, 0, 0)),
        out_shape=jax.ShapeDtypeStruct((B, M, 2 * D_MODEL), BF16),
        compiler_params=_params(1), name="mem_kv",
    )(mem, g, w)


def _xattn_kernel(x_ref, g_ref, wq_ref, kv_ref, wo_ref, o_ref):
    x = x_ref[0]
    q = _dot(_rms(x, g_ref[...]).astype(BF16), wq_ref[...]).astype(BF16)
    outs = []
    for h in range(X_HEADS):
        lo = h * X_HEAD_DIM
        kh = kv_ref[0, :, lo:lo + X_HEAD_DIM]
        vh = kv_ref[0, :, D_MODEL + lo:D_MODEL + lo + X_HEAD_DIM]
        s = _dot_nt(q[:, lo:lo + X_HEAD_DIM], kh) * (X_HEAD_DIM ** -0.5)
        p = jnp.exp(s - jnp.max(s, axis=-1, keepdims=True))
        den = jnp.sum(p, axis=-1, keepdims=True)
        outs.append((_dot(p.astype(BF16), vh) / den).astype(BF16))
    o_ref[0] = x + _dot(jnp.concatenate(outs, axis=1), wo_ref[...])


def _xattn(x, g, wq, kv, wo, tm):
    B, L, _ = x.shape
    row = lambda b, i: (b, i, 0)
    return pl.pallas_call(
        _xattn_kernel,
        grid=(B, L // tm),
        in_specs=[pl.BlockSpec((1, tm, D_MODEL), row), _resident((1, D_MODEL)), _resident(wq.shape),
                  pl.BlockSpec((1, MEM_LEN, 2 * D_MODEL), lambda b, i: (b, 0, 0)),
                  _resident(wo.shape)],
        out_specs=pl.BlockSpec((1, tm, D_MODEL), row),
        out_shape=jax.ShapeDtypeStruct(x.shape, F32),
        compiler_params=_params(2), name="mem_xattn",
    )(x, g, wq, kv, wo)


def _ffn_kernel(x_ref, g_ref, wgu_ref, wd_ref, gf_ref, o_ref, act_ref, *, final_norm):
    x = x_ref[0]
    h = _rms(x, g_ref[...]).astype(BF16)
    for c in range(0, D_FF, MXU_N):
        gate = _dot(h, wgu_ref[:, c:c + MXU_N])
        up = _dot(h, wgu_ref[:, D_FF + c:D_FF + c + MXU_N])
        act_ref[:, c:c + MXU_N] = (gate * _sigmoid(gate) * up).astype(BF16)
    y = x + _dot(act_ref[...], wd_ref[...])
    if final_norm:
        y = _rms(y, gf_ref[...])
    o_ref[0] = y


def _ffn(x, g, wgu, wd, g_final, final_norm, tm):
    B, L, _ = x.shape
    row = lambda b, i: (b, i, 0)
    return pl.pallas_call(
        functools.partial(_ffn_kernel, final_norm=final_norm),
        grid=(B, L // tm),
        in_specs=[pl.BlockSpec((1, tm, D_MODEL), row), _resident((1, D_MODEL)),
                  _resident(wgu.shape), _resident(wd.shape), _resident((1, D_MODEL))],
        out_specs=pl.BlockSpec((1, tm, D_MODEL), row),
        out_shape=jax.ShapeDtypeStruct(x.shape, F32),
        scratch_shapes=[pltpu.VMEM((tm, D_FF), BF16)],
        compiler_params=_params(2), name="swiglu",
    )(x, g, wgu, wd, g_final)


TM = 256


def _trunk(x, mem, layers, g_final):
    for l, p in enumerate(layers):
        pa, pr, pn, gates = _proj(x, p["g_mix"], p["w_in"], TM)
        ya = _wattn(pa, p["a_bias"], p["a_sink"])
        yb = _retention(pr, p["r_lg"], p["r_gchunk"])
        yc = _nattn(pn, p["n_bias"])
        x = _merge(x, ya, yb, yc, gates, p["w_branch"], p["w_mix_out"], TM)
        kv = _memkv(mem, p["g_mem"], p["w_xkv"])
        x = _xattn(x, p["g_xattn"], p["w_xq"], kv, p["w_xo"], TM)
        x = _ffn(x, p["g_ffn"], p["w_gate_up"], p["w_down"], g_final, l == DEPTH - 1, TM)
    return x


def kernel(x_prompt, x_sample, mem_prompt, mem_sample, g_mix, w_in, attn_sink, ret_decay, na_rpb,
           w_branch, w_mix_out, g_xattn, g_mem, w_xq, w_xkv, w_xo, g_ffn, w_gate_up, w_down, g_final):
    layers = []
    for l in range(DEPTH):
        a_bias, a_sink = _wattn_tables(attn_sink[l])
        lg = jax.nn.log_sigmoid(ret_decay[l].astype(F32))
        layers.append(dict(
            g_mix=g_mix[l].reshape(1, D_MODEL), w_in=w_in[l].astype(BF16),
            a_bias=a_bias, a_sink=a_sink,
            r_lg=lg, r_gchunk=jnp.exp(R_CHUNK * lg),
            n_bias=_na_bias(na_rpb[l]),
            w_branch=w_branch[l].astype(BF16), w_mix_out=w_mix_out[l].astype(BF16),
            g_xattn=g_xattn[l].reshape(1, D_MODEL), g_mem=g_mem[l].reshape(1, D_MODEL),
            w_xq=w_xq[l].astype(BF16), w_xkv=w_xkv[l].astype(BF16), w_xo=w_xo[l].astype(BF16),
            g_ffn=g_ffn[l].reshape(1, D_MODEL),
            w_gate_up=w_gate_up[l].astype(BF16), w_down=w_down[l].astype(BF16)))
    gf = g_final.reshape(1, D_MODEL)
    return (_trunk(x_prompt, mem_prompt, layers, gf), _trunk(x_sample, mem_sample, layers, gf))
```

```python
import functools

import numpy as np
import jax
import jax.numpy as jnp
from jax import lax
from jax.experimental import pallas as pl
from jax.experimental.pallas import tpu as pltpu

F32 = jnp.float32
BF16 = jnp.bfloat16

D_MODEL = 1024
DEPTH = 2
EPS = 1e-6
HEAD_DIM = 64
A_HEADS = 8
A_KV_HEADS = 2
A_GROUPS = A_HEADS // A_KV_HEADS
A_WINDOW = 128
A_BLOCK = 128
R_HEADS = 4
R_DK = 128
R_DV = 128
R_CHUNK = 128
N_HEADS = 8
GRID_W = 64
NA_ROWS = 8
NA_COLS = 16
MEM_LEN = 256
X_HEADS = 4
X_HEAD_DIM = D_MODEL // X_HEADS
D_FF = 2816
A_Q = A_HEADS * HEAD_DIM
A_KV = A_KV_HEADS * HEAD_DIM
R_W = R_HEADS * R_DK
N_W = N_HEADS * HEAD_DIM
W_A = A_Q + 2 * A_KV
W_R = 4 * R_W
W_N = 3 * N_W
W_G = 3 * D_MODEL
D_IN = W_A + W_R + W_N + W_G
NEG = -1e30

LANE = 128
MXU_N = 256
VMEM_LIMIT = 56 * 1024 * 1024


def _params(n_axes, flags=None):
    return pltpu.CompilerParams(
        dimension_semantics=("arbitrary",) * n_axes, vmem_limit_bytes=VMEM_LIMIT, flags=flags)


def _resident(shape):
    return pl.BlockSpec(shape, lambda *_: (0,) * len(shape), pipeline_mode=pl.Buffered(1))


def _rms(x, g):
    return x * lax.rsqrt(jnp.mean(x * x, axis=-1, keepdims=True) + EPS) * g


def _sigmoid(x):
    return 1.0 / (1.0 + jnp.exp(-x))


def _dot(a, b):
    return jnp.dot(a, b, preferred_element_type=F32)


def _dot_nt(a, b):
    return lax.dot_general(a, b, (((1,), (1,)), ((), ())), preferred_element_type=F32)


def _dot_tn(a, b):
    return lax.dot_general(a, b, (((0,), (0,)), ((), ())), preferred_element_type=F32)


def _proj_kernel(x_ref, g_ref, w_ref, oa_ref, or_ref, on_ref, og_ref):
    h = _rms(x_ref[0], g_ref[...]).astype(BF16)
    col = 0
    for o_ref, width in ((oa_ref, W_A), (or_ref, W_R), (on_ref, W_N), (og_ref, W_G)):
        for c in range(0, width, 2 * MXU_N):
            n = min(2 * MXU_N, width - c)
            o_ref[0, :, c:c + n] = _dot(h, w_ref[:, col + c:col + c + n]).astype(o_ref.dtype)
        col += width


def _proj(x, g, w, tm):
    B, L, _ = x.shape
    row = lambda b, i: (b, i, 0)
    return pl.pallas_call(
        _proj_kernel,
        grid=(B, L // tm),
        in_specs=[pl.BlockSpec((1, tm, D_MODEL), row), _resident((1, D_MODEL)),
                  _resident((D_MODEL, D_IN))],
        out_specs=[pl.BlockSpec((1, tm, W_A), row), pl.BlockSpec((1, tm, W_R), row),
                   pl.BlockSpec((1, tm, W_N), row), pl.BlockSpec((1, tm, W_G), row)],
        out_shape=[jax.ShapeDtypeStruct((B, L, W_A), BF16), jax.ShapeDtypeStruct((B, L, W_R), F32),
                   jax.ShapeDtypeStruct((B, L, W_N), BF16), jax.ShapeDtypeStruct((B, L, W_G), F32)],
        compiler_params=_params(2), name="proj_in",
    )(x, g, w)


WA_QB = 4


def _wattn_kernel(q_ref, kp_ref, kc_ref, kn_ref, vp_ref, vc_ref, vn_ref, bias_ref, sink_ref,
                  o_ref, *, seq_len):
    n = pl.program_id(1)
    win = 3 * A_BLOCK
    k = jnp.concatenate([kp_ref[0], kc_ref[0], kn_ref[0]], axis=0).astype(F32) * (HEAD_DIM ** -0.5)
    v = jnp.concatenate([vp_ref[0], vc_ref[0], vn_ref[0]], axis=0).astype(F32)
    lo = lax.broadcasted_iota(jnp.int32, k.shape, 1) < HEAD_DIM
    swap = lambda x: jnp.concatenate([x[:, HEAD_DIM:], x[:, :HEAD_DIM]], axis=1)
    ks, vs = swap(k), swap(v)
    k_lo = [jnp.where(lo, k, 0.0).astype(BF16), jnp.where(lo, ks, 0.0).astype(BF16)]
    k_hi = [jnp.where(lo, 0.0, ks).astype(BF16), jnp.where(lo, 0.0, k).astype(BF16)]
    v_lo = [jnp.where(lo, v, 0.0).astype(BF16), jnp.where(lo, vs, 0.0).astype(BF16)]
    v_hi = [jnp.where(lo, 0.0, vs).astype(BF16), jnp.where(lo, 0.0, v).astype(BF16)]
    j = lax.broadcasted_iota(jnp.int32, (win, 1), 0)
    for i in range(WA_QB):
        kpos = (n * WA_QB + i - 1) * A_BLOCK + j
        kvalid = (kpos >= 0) & (kpos < seq_len)
        rows = slice(i * A_BLOCK, (i + 1) * A_BLOCK)
        wrows = slice(i * A_BLOCK, i * A_BLOCK + win)
        for h in range(A_KV_HEADS):
            c0 = h * A_GROUPS * HEAD_DIM
            qh = jnp.concatenate([q_ref[0, rows, c0:c0 + 2 * HEAD_DIM],
                                  q_ref[0, rows, c0 + 2 * HEAD_DIM:c0 + 4 * HEAD_DIM]], axis=0)
            probs = []
            for par, kw in enumerate((k_lo[h][wrows], k_hi[h][wrows])):
                s = _dot_nt(kw, qh) + bias_ref[h, par]
                if i in (0, WA_QB - 1):
                    s = jnp.where(kvalid, s, NEG)
                sk = sink_ref[h, par]
                m = jnp.maximum(jnp.max(s, axis=0, keepdims=True), sk)
                p = jnp.exp(s - m)
                den = jnp.sum(p, axis=0, keepdims=True) + jnp.exp(sk - m)
                probs.append((p * (1.0 / den)).astype(BF16))
            v2 = jnp.concatenate([v_lo[h][wrows], v_hi[h][wrows]], axis=0)
            res = _dot_tn(jnp.concatenate(probs, axis=0), v2)
            o_ref[0, rows, c0:c0 + 4 * HEAD_DIM] = jnp.concatenate(
                [res[:A_BLOCK], res[A_BLOCK:]], axis=1).astype(o_ref.dtype)


def _wattn(pa, bias, sink):
    B, L, _ = pa.shape
    nb = L // A_BLOCK
    assert nb % WA_QB == 0
    kcol, vcol = A_Q // A_KV, A_Q // A_KV + 1
    prev = lambda c: pl.BlockSpec((1, A_BLOCK, A_KV), lambda b, n: (b, jnp.maximum(n * WA_QB - 1, 0), c))
    cur = lambda c: pl.BlockSpec((1, WA_QB * A_BLOCK, A_KV), lambda b, n: (b, n, c))
    nxt = lambda c: pl.BlockSpec((1, A_BLOCK, A_KV),
                                 lambda b, n: (b, jnp.minimum((n + 1) * WA_QB, nb - 1), c))
    return pl.pallas_call(
        functools.partial(_wattn_kernel, seq_len=L),
        grid=(B, nb // WA_QB),
        in_specs=[pl.BlockSpec((1, WA_QB * A_BLOCK, A_Q), lambda b, n: (b, n, 0)),
                  prev(kcol), cur(kcol), nxt(kcol), prev(vcol), cur(vcol), nxt(vcol),
                  _resident(bias.shape), _resident(sink.shape)],
        out_specs=pl.BlockSpec((1, WA_QB * A_BLOCK, A_Q), lambda b, n: (b, n, 0)),
        out_shape=jax.ShapeDtypeStruct((B, L, A_Q), BF16),
        compiler_params=_params(2), name="window_gqa",
    )(pa, pa, pa, pa, pa, pa, pa, bias, sink)


def _wattn_tables(attn_sink):
    i = np.arange(A_BLOCK)[None, :]
    j = np.arange(3 * A_BLOCK)[:, None]
    dist = np.abs(A_BLOCK + i - j)
    slopes = jnp.exp2(-8.0 * jnp.arange(1, A_HEADS + 1, dtype=F32) / A_HEADS)
    bias = -slopes[:, None, None] * jnp.asarray(dist, F32)[None]
    bias = jnp.where(jnp.asarray(dist <= A_WINDOW)[None], bias, NEG)
    bias = bias.reshape(A_KV_HEADS, 2, 2, 3 * A_BLOCK, A_BLOCK).transpose(0, 2, 3, 1, 4)
    bias = bias.reshape(A_KV_HEADS, 2, 3 * A_BLOCK, 2 * A_BLOCK)
    sink = jnp.repeat(attn_sink.astype(F32), A_BLOCK).reshape(A_KV_HEADS, 2, 2, A_BLOCK)
    sink = sink.transpose(0, 2, 1, 3).reshape(A_KV_HEADS, 2, 1, 2 * A_BLOCK)
    return bias, sink


RET_UNROLL = 8


def _ret_kernel(lg_ref, gc_ref, q_ref, k_ref, v_ref, g_ref, o_ref, st_ref, *, nc):
    h = pl.program_id(1)
    C = R_CHUNK
    lgf, lgb = lg_ref[0, h], lg_ref[1, h]
    gcf, gcb = gc_ref[0, h], gc_ref[1, h]
    diff = (lax.broadcasted_iota(jnp.int32, (C, C), 0)
            - lax.broadcasted_iota(jnp.int32, (C, C), 1)).astype(F32)
    decay = (jnp.where(diff >= 0, jnp.exp(jnp.maximum(diff, 0.0) * lgf), 0.0)
             + jnp.where(diff <= 0, jnp.exp(jnp.maximum(-diff, 0.0) * lgb), 0.0))
    idx = lax.broadcasted_iota(jnp.int32, (C, R_DK), 0).astype(F32)
    wq2 = jnp.concatenate([jnp.exp((idx + 1.0) * lgf), jnp.exp((C - idx) * lgb)], axis=1)
    wk2 = jnp.concatenate([jnp.exp((C - 1.0 - idx) * lgf), jnp.exp(idx * lgb)], axis=1)

    def rows(n):
        return pl.ds(pl.multiple_of(n * C, C), C)

    def kv_body(i, carry):
        for u in range(RET_UNROLL):
            n = i * RET_UNROLL + u
            k = k_ref[0, rows(n), :] * (R_DK ** -0.5)
            kw = (jnp.concatenate([k, k], axis=1) * wk2).astype(BF16)
            st_ref[n] = _dot_tn(kw, v_ref[0, rows(n), :].astype(BF16))
        return carry

    lax.fori_loop(0, nc // RET_UNROLL, kv_body, 0)

    def scan_body(t, carry):
        sf, sb = carry
        nb = nc - 1 - t
        kvf = st_ref[t, :R_DK, :]
        st_ref[t, :R_DK, :] = sf
        kvb = st_ref[nb, R_DK:, :]
        st_ref[nb, R_DK:, :] = sb
        return sf * gcf + kvf, sb * gcb + kvb

    zero = jnp.zeros((R_DK, R_DV), F32)
    lax.fori_loop(0, nc, scan_body, (zero, zero))

    def out_body(i, carry):
        for u in range(RET_UNROLL):
            n = i * RET_UNROLL + u
            q = q_ref[0, rows(n), :]
            k = k_ref[0, rows(n), :] * (R_DK ** -0.5)
            inner = _dot_nt(q.astype(BF16), k.astype(BF16)) * decay
            qw = jnp.concatenate([q, q], axis=1) * wq2
            lhs = jnp.concatenate([inner, qw], axis=1).astype(BF16)
            rhs = jnp.concatenate([v_ref[0, rows(n), :].astype(BF16), st_ref[n].astype(BF16)], axis=0)
            o = _dot(lhs, rhs)
            o = o * lax.rsqrt(jnp.mean(o * o, axis=-1, keepdims=True) + EPS)
            g = g_ref[0, rows(n), :]
            o_ref[0, rows(n), :] = (g * _sigmoid(g) * o).astype(o_ref.dtype)
        return carry

    lax.fori_loop(0, nc // RET_UNROLL, out_body, 0)


def _retention(pr, lg, gchunk):
    B, L, _ = pr.shape
    nc = L // R_CHUNK
    assert nc % RET_UNROLL == 0
    part = lambda p: pl.BlockSpec((1, L, R_DK), lambda b, h: (b, 0, p * R_HEADS + h))
    smem = pl.BlockSpec(memory_space=pltpu.SMEM)
    return pl.pallas_call(
        functools.partial(_ret_kernel, nc=nc),
        grid=(B, R_HEADS),
        in_specs=[smem, smem, part(0), part(1), part(2), part(3)],
        out_specs=pl.BlockSpec((1, L, R_DV), lambda b, h: (b, 0, h)),
        out_shape=jax.ShapeDtypeStruct((B, L, R_W), BF16),
        scratch_shapes=[pltpu.VMEM((nc, 2 * R_DK, R_DV), F32)],
        compiler_params=_params(2), name="retention",
    )(lg, gchunk, pr, pr, pr, pr)


NA_UNROLL = 16


def _na_kernel(q_ref, k_ref, v_ref, bias_ref, o_ref, *, n_rows):
    keys = NA_ROWS * GRID_W
    head0 = lax.broadcasted_iota(jnp.int32, (GRID_W, 2 * HEAD_DIM), 1) < HEAD_DIM

    def one_row(r):
        rs = jnp.clip(r - NA_ROWS // 2, 0, n_rows - NA_ROWS)
        qrows = pl.ds(pl.multiple_of(r * GRID_W, GRID_W), GRID_W)
        krows = pl.ds(pl.multiple_of(rs * GRID_W, GRID_W), keys)
        q = q_ref[0, qrows, :].astype(F32) * (HEAD_DIM ** -0.5)
        q2 = jnp.concatenate([jnp.where(head0, q, 0.0), jnp.where(head0, 0.0, q)], axis=0).astype(BF16)
        s = _dot_nt(k_ref[0, krows, :], q2)
        s = s + jnp.concatenate(
            [bias_ref[0, rs - r + (NA_ROWS - 1) + j] for j in range(NA_ROWS)], axis=0)
        p = jnp.exp(s - jnp.max(s, axis=0, keepdims=True))
        p = p * (1.0 / jnp.sum(p, axis=0, keepdims=True))
        o = _dot_tn(p.astype(BF16), v_ref[0, krows, :])
        o_ref[0, qrows, :] = jnp.where(head0, o[:GRID_W], o[GRID_W:]).astype(o_ref.dtype)

    def body(i, carry):
        for u in range(NA_UNROLL):
            one_row(i * NA_UNROLL + u)
        return carry

    lax.fori_loop(0, n_rows // NA_UNROLL, body, 0)


def _nattn(pn, bias):
    B, L, _ = pn.shape
    pairs = N_HEADS // 2
    assert (L // GRID_W) % NA_UNROLL == 0
    part = lambda p: pl.BlockSpec((1, L, 2 * HEAD_DIM), lambda b, hp: (b, 0, p * pairs + hp))
    return pl.pallas_call(
        functools.partial(_na_kernel, n_rows=L // GRID_W),
        grid=(B, pairs),
        in_specs=[part(0), part(1), part(2),
                  pl.BlockSpec((1,) + bias.shape[1:], lambda b, hp: (hp, 0, 0, 0))],
        out_specs=pl.BlockSpec((1, L, 2 * HEAD_DIM), lambda b, hp: (b, 0, hp)),
        out_shape=jax.ShapeDtypeStruct((B, L, N_W), BF16),
        compiler_params=_params(2), name="neighbourhood_attn",
    )(pn, pn, pn, bias)


def _na_bias(rpb):
    c = np.arange(GRID_W)
    cs = np.clip(c - NA_COLS // 2, 0, GRID_W - NA_COLS)
    valid = (c[:, None] >= cs[None, :]) & (c[:, None] < cs[None, :] + NA_COLS)
    dc = c[:, None] - c[None, :] + NA_COLS - 1
    onehot = valid[None] & (dc[None] == np.arange(2 * NA_COLS - 1)[:, None, None])
    t = jnp.where(jnp.asarray(onehot)[None, None], rpb.astype(F32)[:, :, :, None, None], 0.0).sum(axis=2)
    t = jnp.where(jnp.asarray(valid)[None, None], t, NEG)
    t = t.reshape(N_HEADS // 2, 2, 2 * NA_ROWS - 1, GRID_W, GRID_W)
    return t.transpose(0, 2, 3, 1, 4).reshape(N_HEADS // 2, 2 * NA_ROWS - 1, GRID_W, 2 * GRID_W)


def _merge_kernel(x_ref, ya_ref, yb_ref, yc_ref, gate_ref, wb_ref, wo_ref, o_ref):
    merged = None
    for i, y_ref in enumerate((ya_ref, yb_ref, yc_ref)):
        gate = _sigmoid(gate_ref[0, :, i * D_MODEL:(i + 1) * D_MODEL])
        t = gate * _dot(y_ref[0], wb_ref[i])
        merged = t if merged is None else merged + t
    o_ref[0] = x_ref[0] + _dot(merged.astype(BF16), wo_ref[...])


def _merge(x, ya, yb, yc, gates, wb, wo, tm):
    B, L, _ = x.shape
    row = lambda b, i: (b, i, 0)
    y_spec = pl.BlockSpec((1, tm, 512), row)
    return pl.pallas_call(
        _merge_kernel,
        grid=(B, L // tm),
        in_specs=[pl.BlockSpec((1, tm, D_MODEL), row), y_spec, y_spec, y_spec,
                  pl.BlockSpec((1, tm, W_G), row), _resident(wb.shape), _resident(wo.shape)],
        out_specs=pl.BlockSpec((1, tm, D_MODEL), row),
        out_shape=jax.ShapeDtypeStruct(x.shape, F32),
        compiler_params=_params(2), name="merge_out",
    )(x, ya, yb, yc, gates, wb, wo)


def _memkv_kernel(m_ref, g_ref, w_ref, o_ref):
    h = _rms(m_ref[0], g_ref[...]).astype(BF16)
    for c in range(0, 2 * D_MODEL, 2 * MXU_N):
        o_ref[0, :, c:c + 2 * MXU_N] = _dot(h, w_ref[:, c:c + 2 * MXU_N]).astype(o_ref.dtype)


def _memkv(mem, g, w):
    B, M, _ = mem.shape
    return pl.pallas_call(
        _memkv_kernel,
        grid=(B,),
        in_specs=[pl.BlockSpec((1, M, D_MODEL), lambda b: (b, 0, 0)), _resident((1, D_MODEL)),
                  _resident(w.shape)],
        out_specs=pl.BlockSpec((1, M, 2 * D_MODEL), lambda b: (b, 0, 0)),
        out_shape=jax.ShapeDtypeStruct((B, M, 2 * D_MODEL), BF16),
        compiler_params=_params(1), name="mem_kv",
    )(mem, g, w)


def _xattn_kernel(x_ref, g_ref, wq_ref, kv_ref, wo_ref, o_ref):
    x = x_ref[0]
    q = _dot(_rms(x, g_ref[...]).astype(BF16), wq_ref[...]).astype(BF16)
    outs = []
    for h in range(X_HEADS):
        lo = h * X_HEAD_DIM
        kh = kv_ref[0, :, lo:lo + X_HEAD_DIM]
        vh = kv_ref[0, :, D_MODEL + lo:D_MODEL + lo + X_HEAD_DIM]
        s = _dot_nt(q[:, lo:lo + X_HEAD_DIM], kh) * (X_HEAD_DIM ** -0.5)
        p = jnp.exp(s - jnp.max(s, axis=-1, keepdims=True))
        den = jnp.sum(p, axis=-1, keepdims=True)
        outs.append((_dot(p.astype(BF16), vh) / den).astype(BF16))
    o_ref[0] = x + _dot(jnp.concatenate(outs, axis=1), wo_ref[...])


def _xattn(x, g, wq, kv, wo, tm):
    B, L, _ = x.shape
    row = lambda b, i: (b, i, 0)
    return pl.pallas_call(
        _xattn_kernel,
        grid=(B, L // tm),
        in_specs=[pl.BlockSpec((1, tm, D_MODEL), row), _resident((1, D_MODEL)), _resident(wq.shape),
                  pl.BlockSpec((1, MEM_LEN, 2 * D_MODEL), lambda b, i: (b, 0, 0)),
                  _resident(wo.shape)],
        out_specs=pl.BlockSpec((1, tm, D_MODEL), row),
        out_shape=jax.ShapeDtypeStruct(x.shape, F32),
        compiler_params=_params(2), name="mem_xattn",
    )(x, g, wq, kv, wo)


def _ffn_kernel(x_ref, g_ref, wgu_ref, wd_ref, gf_ref, o_ref, act_ref, *, final_norm):
    x = x_ref[0]
    h = _rms(x, g_ref[...]).astype(BF16)
    for c in range(0, D_FF, MXU_N):
        gate = _dot(h, wgu_ref[:, c:c + MXU_N])
        up = _dot(h, wgu_ref[:, D_FF + c:D_FF + c + MXU_N])
        act_ref[:, c:c + MXU_N] = (gate * _sigmoid(gate) * up).astype(BF16)
    y = x + _dot(act_ref[...], wd_ref[...])
    if final_norm:
        y = _rms(y, gf_ref[...])
    o_ref[0] = y


def _ffn(x, g, wgu, wd, g_final, final_norm, tm):
    B, L, _ = x.shape
    row = lambda b, i: (b, i, 0)
    return pl.pallas_call(
        functools.partial(_ffn_kernel, final_norm=final_norm),
        grid=(B, L // tm),
        in_specs=[pl.BlockSpec((1, tm, D_MODEL), row), _resident((1, D_MODEL)),
                  _resident(wgu.shape), _resident(wd.shape), _resident((1, D_MODEL))],
        out_specs=pl.BlockSpec((1, tm, D_MODEL), row),
        out_shape=jax.ShapeDtypeStruct(x.shape, F32),
        scratch_shapes=[pltpu.VMEM((tm, D_FF), BF16)],
        compiler_params=_params(2), name="swiglu",
    )(x, g, wgu, wd, g_final)


TM = 256


def _trunk(x, mem, layers, g_final):
    for l, p in enumerate(layers):
        pa, pr, pn, gates = _proj(x, p["g_mix"], p["w_in"], TM)
        ya = _wattn(pa, p["a_bias"], p["a_sink"])
        yb = _retention(pr, p["r_lg"], p["r_gchunk"])
        yc = _nattn(pn, p["n_bias"])
        x = _merge(x, ya, yb, yc, gates, p["w_branch"], p["w_mix_out"], TM)
        kv = _memkv(mem, p["g_mem"], p["w_xkv"])
        x = _xattn(x, p["g_xattn"], p["w_xq"], kv, p["w_xo"], TM)
        x = _ffn(x, p["g_ffn"], p["w_gate_up"], p["w_down"], g_final, l == DEPTH - 1, TM)
    return x


def kernel(x_prompt, x_sample, mem_prompt, mem_sample, g_mix, w_in, attn_sink, ret_decay, na_rpb,
           w_branch, w_mix_out, g_xattn, g_mem, w_xq, w_xkv, w_xo, g_ffn, w_gate_up, w_down, g_final):
    layers = []
    for l in range(DEPTH):
        a_bias, a_sink = _wattn_tables(attn_sink[l])
        lg = jax.nn.log_sigmoid(ret_decay[l].astype(F32))
        layers.append(dict(
            g_mix=g_mix[l].reshape(1, D_MODEL), w_in=w_in[l].astype(BF16),
            a_bias=a_bias, a_sink=a_sink,
            r_lg=lg, r_gchunk=jnp.exp(R_CHUNK * lg),
            n_bias=_na_bias(na_rpb[l]),
            w_branch=w_branch[l].astype(BF16), w_mix_out=w_mix_out[l].astype(BF16),
            g_xattn=g_xattn[l].reshape(1, D_MODEL), g_mem=g_mem[l].reshape(1, D_MODEL),
            w_xq=w_xq[l].astype(BF16), w_xkv=w_xkv[l].astype(BF16), w_xo=w_xo[l].astype(BF16),
            g_ffn=g_ffn[l].reshape(1, D_MODEL),
            w_gate_up=w_gate_up[l].astype(BF16), w_down=w_down[l].astype(BF16)))
    gf = g_final.reshape(1, D_MODEL)
    return (_trunk(x_prompt, mem_prompt, layers, gf), _trunk(x_sample, mem_sample, layers, gf))
```

```python
import functools

import numpy as np
import jax
import jax.numpy as jnp
from jax import lax
from jax.experimental import pallas as pl
from jax.experimental.pallas import tpu as pltpu

F32 = jnp.float32
BF16 = jnp.bfloat16

D_MODEL = 1024
DEPTH = 2
EPS = 1e-6
HEAD_DIM = 64
A_HEADS = 8
A_KV_HEADS = 2
A_GROUPS = A_HEADS // A_KV_HEADS
A_WINDOW = 128
A_BLOCK = 128
R_HEADS = 4
R_DK = 128
R_DV = 128
R_CHUNK = 128
N_HEADS = 8
GRID_W = 64
NA_ROWS = 8
NA_COLS = 16
MEM_LEN = 256
X_HEADS = 4
X_HEAD_DIM = D_MODEL // X_HEADS
D_FF = 2816
A_Q = A_HEADS * HEAD_DIM
A_KV = A_KV_HEADS * HEAD_DIM
R_W = R_HEADS * R_DK
N_W = N_HEADS * HEAD_DIM
W_A = A_Q + 2 * A_KV
W_R = 4 * R_W
W_N = 3 * N_W
W_MIX = W_A + W_R + W_N
W_G = 3 * D_MODEL
NEG = -1e30

LANE = 128
MXU_N = 256
VMEM_LIMIT = 56 * 1024 * 1024


def _params(n_axes, flags=None):
    return pltpu.CompilerParams(
        dimension_semantics=("arbitrary",) * n_axes, vmem_limit_bytes=VMEM_LIMIT, flags=flags)


def _resident(shape):
    return pl.BlockSpec(shape, lambda *_: (0,) * len(shape), pipeline_mode=pl.Buffered(1))


def _rms(x, g):
    return x * lax.rsqrt(jnp.mean(x * x, axis=-1, keepdims=True) + EPS) * g


def _sigmoid(x):
    return 1.0 / (1.0 + jnp.exp(-x))


def _dot(a, b):
    return jnp.dot(a, b, preferred_element_type=F32)


def _dot_nt(a, b):
    return lax.dot_general(a, b, (((1,), (1,)), ((), ())), preferred_element_type=F32)


def _dot_tn(a, b):
    return lax.dot_general(a, b, (((0,), (0,)), ((), ())), preferred_element_type=F32)


def _proj_kernel(x_ref, g_ref, w_ref, oa_ref, or_ref, on_ref):
    h = _rms(x_ref[0], g_ref[...]).astype(BF16)
    col = 0
    for o_ref, width in ((oa_ref, W_A), (or_ref, W_R), (on_ref, W_N)):
        for c in range(0, width, 2 * MXU_N):
            n = min(2 * MXU_N, width - c)
            o_ref[0, :, c:c + n] = _dot(h, w_ref[:, col + c:col + c + n]).astype(o_ref.dtype)
        col += width


def _proj(x, g, w, tm):
    B, L, _ = x.shape
    row = lambda b, i: (b, i, 0)
    return pl.pallas_call(
        _proj_kernel,
        grid=(B, L // tm),
        in_specs=[pl.BlockSpec((1, tm, D_MODEL), row), _resident((1, D_MODEL)),
                  _resident((D_MODEL, W_MIX))],
        out_specs=[pl.BlockSpec((1, tm, W_A), row), pl.BlockSpec((1, tm, W_R), row),
                   pl.BlockSpec((1, tm, W_N), row)],
        out_shape=[jax.ShapeDtypeStruct((B, L, W_A), BF16), jax.ShapeDtypeStruct((B, L, W_R), F32),
                   jax.ShapeDtypeStruct((B, L, W_N), BF16)],
        compiler_params=_params(2), name="proj_in",
    )(x, g, w)


WA_QB = 4
WA_STAGE = 4


def _wattn_kernel(q_ref, kp_ref, kc_ref, kn_ref, vp_ref, vc_ref, vn_ref, bias_ref, sink_ref,
                  o_ref, *, seq_len):
    n = pl.program_id(1)
    win = 3 * A_BLOCK
    k = jnp.concatenate([kp_ref[0], kc_ref[0], kn_ref[0]], axis=0).astype(F32) * (HEAD_DIM ** -0.5)
    v = jnp.concatenate([vp_ref[0], vc_ref[0], vn_ref[0]], axis=0).astype(F32)
    lo = lax.broadcasted_iota(jnp.int32, k.shape, 1) < HEAD_DIM
    swap = lambda x: jnp.concatenate([x[:, HEAD_DIM:], x[:, :HEAD_DIM]], axis=1)
    ks, vs = swap(k), swap(v)
    k_lo = [jnp.where(lo, k, 0.0).astype(BF16), jnp.where(lo, ks, 0.0).astype(BF16)]
    k_hi = [jnp.where(lo, 0.0, ks).astype(BF16), jnp.where(lo, 0.0, k).astype(BF16)]
    v_lo = [jnp.where(lo, v, 0.0).astype(BF16), jnp.where(lo, vs, 0.0).astype(BF16)]
    v_hi = [jnp.where(lo, 0.0, vs).astype(BF16), jnp.where(lo, 0.0, v).astype(BF16)]
    j = lax.broadcasted_iota(jnp.int32, (win, 1), 0)
    units = [(i, h) for i in range(WA_QB) for h in range(A_KV_HEADS)]
    wrows = lambda i: slice(i * A_BLOCK, i * A_BLOCK + win)

    def scores(i, h):
        rows = slice(i * A_BLOCK, (i + 1) * A_BLOCK)
        c0 = h * A_GROUPS * HEAD_DIM
        qh = jnp.concatenate([q_ref[0, rows, c0:c0 + 2 * HEAD_DIM],
                              q_ref[0, rows, c0 + 2 * HEAD_DIM:c0 + 4 * HEAD_DIM]], axis=0)
        return [_dot_nt(kw[wrows(i)], qh) for kw in (k_lo[h], k_hi[h])]

    def softmax(i, h, par, s):
        s = s + bias_ref[h, par]
        if i in (0, WA_QB - 1):
            kpos = (n * WA_QB + i - 1) * A_BLOCK + j
            s = jnp.where((kpos >= 0) & (kpos < seq_len), s, NEG)
        sk = sink_ref[h, par]
        m = jnp.maximum(jnp.max(s, axis=0, keepdims=True), sk)
        p = jnp.exp(s - m)
        den = jnp.sum(p, axis=0, keepdims=True) + jnp.exp(sk - m)
        return (p * (1.0 / den)).astype(BF16)

    for g0 in range(0, len(units), WA_STAGE):
        grp = units[g0:g0 + WA_STAGE]
        all_s = [scores(i, h) for i, h in grp]
        all_p = [jnp.concatenate([softmax(i, h, par, s) for par, s in enumerate(ss)], axis=0)
                 for (i, h), ss in zip(grp, all_s)]
        all_o = [_dot_tn(p, jnp.concatenate([v_lo[h][wrows(i)], v_hi[h][wrows(i)]], axis=0))
                 for (i, h), p in zip(grp, all_p)]
        for (i, h), res in zip(grp, all_o):
            c0 = h * A_GROUPS * HEAD_DIM
            o_ref[0, i * A_BLOCK:(i + 1) * A_BLOCK, c0:c0 + 4 * HEAD_DIM] = jnp.concatenate(
                [res[:A_BLOCK], res[A_BLOCK:]], axis=1).astype(o_ref.dtype)


def _wattn(pa, bias, sink):
    B, L, _ = pa.shape
    nb = L // A_BLOCK
    assert nb % WA_QB == 0
    kcol, vcol = A_Q // A_KV, A_Q // A_KV + 1
    prev = lambda c: pl.BlockSpec((1, A_BLOCK, A_KV), lambda b, n: (b, jnp.maximum(n * WA_QB - 1, 0), c))
    cur = lambda c: pl.BlockSpec((1, WA_QB * A_BLOCK, A_KV), lambda b, n: (b, n, c))
    nxt = lambda c: pl.BlockSpec((1, A_BLOCK, A_KV),
                                 lambda b, n: (b, jnp.minimum((n + 1) * WA_QB, nb - 1), c))
    return pl.pallas_call(
        functools.partial(_wattn_kernel, seq_len=L),
        grid=(B, nb // WA_QB),
        in_specs=[pl.BlockSpec((1, WA_QB * A_BLOCK, A_Q), lambda b, n: (b, n, 0)),
                  prev(kcol), cur(kcol), nxt(kcol), prev(vcol), cur(vcol), nxt(vcol),
                  _resident(bias.shape), _resident(sink.shape)],
        out_specs=pl.BlockSpec((1, WA_QB * A_BLOCK, A_Q), lambda b, n: (b, n, 0)),
        out_shape=jax.ShapeDtypeStruct((B, L, A_Q), BF16),
        compiler_params=_params(2), name="window_gqa",
    )(pa, pa, pa, pa, pa, pa, pa, bias, sink)


def _wattn_tables(attn_sink):
    i = np.arange(A_BLOCK)[None, :]
    j = np.arange(3 * A_BLOCK)[:, None]
    dist = np.abs(A_BLOCK + i - j)
    slopes = jnp.exp2(-8.0 * jnp.arange(1, A_HEADS + 1, dtype=F32) / A_HEADS)
    bias = -slopes[:, None, None] * jnp.asarray(dist, F32)[None]
    bias = jnp.where(jnp.asarray(dist <= A_WINDOW)[None], bias, NEG)
    bias = bias.reshape(A_KV_HEADS, 2, 2, 3 * A_BLOCK, A_BLOCK).transpose(0, 2, 3, 1, 4)
    bias = bias.reshape(A_KV_HEADS, 2, 3 * A_BLOCK, 2 * A_BLOCK)
    sink = jnp.repeat(attn_sink.astype(F32), A_BLOCK).reshape(A_KV_HEADS, 2, 2, A_BLOCK)
    sink = sink.transpose(0, 2, 1, 3).reshape(A_KV_HEADS, 2, 1, 2 * A_BLOCK)
    return bias, sink


RET_UNROLL = 8


def _ret_kernel(lg_ref, gc_ref, q_ref, k_ref, v_ref, g_ref, o_ref, st_ref, *, nc):
    h = pl.program_id(1)
    C = R_CHUNK
    lgf, lgb = lg_ref[0, h], lg_ref[1, h]
    gcf, gcb = gc_ref[0, h], gc_ref[1, h]
    diff = (lax.broadcasted_iota(jnp.int32, (C, C), 0)
            - lax.broadcasted_iota(jnp.int32, (C, C), 1)).astype(F32)
    decay = (jnp.where(diff >= 0, jnp.exp(jnp.maximum(diff, 0.0) * lgf), 0.0)
             + jnp.where(diff <= 0, jnp.exp(jnp.maximum(-diff, 0.0) * lgb), 0.0))
    idx = lax.broadcasted_iota(jnp.int32, (C, R_DK), 0).astype(F32)
    wq2 = jnp.concatenate([jnp.exp((idx + 1.0) * lgf), jnp.exp((C - idx) * lgb)], axis=1)
    wk2 = jnp.concatenate([jnp.exp((C - 1.0 - idx) * lgf), jnp.exp(idx * lgb)], axis=1)

    def rows(n):
        return pl.ds(pl.multiple_of(n * C, C), C)

    def kv_body(i, carry):
        for u in range(RET_UNROLL):
            n = i * RET_UNROLL + u
            k = k_ref[0, rows(n), :] * (R_DK ** -0.5)
            kw = (jnp.concatenate([k, k], axis=1) * wk2).astype(BF16)
            st_ref[n] = _dot_tn(kw, v_ref[0, rows(n), :].astype(BF16))
        return carry

    lax.fori_loop(0, nc // RET_UNROLL, kv_body, 0)

    def scan_body(t, carry):
        sf, sb = carry
        nb = nc - 1 - t
        kvf = st_ref[t, :R_DK, :]
        st_ref[t, :R_DK, :] = sf
        kvb = st_ref[nb, R_DK:, :]
        st_ref[nb, R_DK:, :] = sb
        return sf * gcf + kvf, sb * gcb + kvb

    zero = jnp.zeros((R_DK, R_DV), F32)
    lax.fori_loop(0, nc, scan_body, (zero, zero))

    def out_body(i, carry):
        ns = [i * RET_UNROLL + u for u in range(RET_UNROLL)]
        qs = [q_ref[0, rows(n), :] for n in ns]
        inner = [_dot_nt(q.astype(BF16), (k_ref[0, rows(n), :] * (R_DK ** -0.5)).astype(BF16))
                 for n, q in zip(ns, qs)]
        lhs = [jnp.concatenate([s * decay, jnp.concatenate([q, q], axis=1) * wq2], axis=1).astype(BF16)
               for s, q in zip(inner, qs)]
        outs = [_dot(a, jnp.concatenate([v_ref[0, rows(n), :].astype(BF16),
                                         st_ref[n].astype(BF16)], axis=0))
                for n, a in zip(ns, lhs)]
        for n, o in zip(ns, outs):
            o = o * lax.rsqrt(jnp.mean(o * o, axis=-1, keepdims=True) + EPS)
            g = g_ref[0, rows(n), :]
            o_ref[0, rows(n), :] = (g * _sigmoid(g) * o).astype(o_ref.dtype)
        return carry

    lax.fori_loop(0, nc // RET_UNROLL, out_body, 0)


def _retention(pr, lg, gchunk):
    B, L, _ = pr.shape
    nc = L // R_CHUNK
    assert nc % RET_UNROLL == 0
    part = lambda p: pl.BlockSpec((1, L, R_DK), lambda b, h: (b, 0, p * R_HEADS + h))
    smem = pl.BlockSpec(memory_space=pltpu.SMEM)
    return pl.pallas_call(
        functools.partial(_ret_kernel, nc=nc),
        grid=(B, R_HEADS),
        in_specs=[smem, smem, part(0), part(1), part(2), part(3)],
        out_specs=pl.BlockSpec((1, L, R_DV), lambda b, h: (b, 0, h)),
        out_shape=jax.ShapeDtypeStruct((B, L, R_W), BF16),
        scratch_shapes=[pltpu.VMEM((nc, 2 * R_DK, R_DV), F32)],
        compiler_params=_params(2), name="retention",
    )(lg, gchunk, pr, pr, pr, pr)


NA_UNROLL = 16


def _na_kernel(q_ref, k_ref, v_ref, bias_ref, o_ref, *, n_rows):
    keys = NA_ROWS * GRID_W
    head0 = lax.broadcasted_iota(jnp.int32, (GRID_W, 2 * HEAD_DIM), 1) < HEAD_DIM

    def qrows(r):
        return pl.ds(pl.multiple_of(r * GRID_W, GRID_W), GRID_W)

    def krows(r):
        rs = jnp.clip(r - NA_ROWS // 2, 0, n_rows - NA_ROWS)
        return pl.ds(pl.multiple_of(rs * GRID_W, GRID_W), keys)

    def scores(r):
        q = q_ref[0, qrows(r), :].astype(F32) * (HEAD_DIM ** -0.5)
        q2 = jnp.concatenate([jnp.where(head0, q, 0.0), jnp.where(head0, 0.0, q)], axis=0).astype(BF16)
        return _dot_nt(k_ref[0, krows(r), :], q2)

    def softmax(r, s):
        rs = jnp.clip(r - NA_ROWS // 2, 0, n_rows - NA_ROWS)
        s = s + jnp.concatenate(
            [bias_ref[0, rs - r + (NA_ROWS - 1) + j] for j in range(NA_ROWS)], axis=0)
        p = jnp.exp(s - jnp.max(s, axis=0, keepdims=True))
        return (p * (1.0 / jnp.sum(p, axis=0, keepdims=True))).astype(BF16)

    def body(i, carry):
        rs_ = [i * NA_UNROLL + u for u in range(NA_UNROLL)]
        all_s = [scores(r) for r in rs_]
        all_p = [softmax(r, s) for r, s in zip(rs_, all_s)]
        all_o = [_dot_tn(p, v_ref[0, krows(r), :]) for r, p in zip(rs_, all_p)]
        for r, o in zip(rs_, all_o):
            o_ref[0, qrows(r), :] = jnp.where(head0, o[:GRID_W], o[GRID_W:]).astype(o_ref.dtype)
        return carry

    lax.fori_loop(0, n_rows // NA_UNROLL, body, 0)


def _nattn(pn, bias):
    B, L, _ = pn.shape
    pairs = N_HEADS // 2
    assert (L // GRID_W) % NA_UNROLL == 0
    part = lambda p: pl.BlockSpec((1, L, 2 * HEAD_DIM), lambda b, hp: (b, 0, p * pairs + hp))
    return pl.pallas_call(
        functools.partial(_na_kernel, n_rows=L // GRID_W),
        grid=(B, pairs),
        in_specs=[part(0), part(1), part(2),
                  pl.BlockSpec((1,) + bias.shape[1:], lambda b, hp: (hp, 0, 0, 0))],
        out_specs=pl.BlockSpec((1, L, 2 * HEAD_DIM), lambda b, hp: (b, 0, hp)),
        out_shape=jax.ShapeDtypeStruct((B, L, N_W), BF16),
        compiler_params=_params(2), name="neighbourhood_attn",
    )(pn, pn, pn, bias)


def _na_bias(rpb):
    c = np.arange(GRID_W)
    cs = np.clip(c - NA_COLS // 2, 0, GRID_W - NA_COLS)
    valid = (c[:, None] >= cs[None, :]) & (c[:, None] < cs[None, :] + NA_COLS)
    dc = c[:, None] - c[None, :] + NA_COLS - 1
    onehot = valid[None] & (dc[None] == np.arange(2 * NA_COLS - 1)[:, None, None])
    t = jnp.where(jnp.asarray(onehot)[None, None], rpb.astype(F32)[:, :, :, None, None], 0.0).sum(axis=2)
    t = jnp.where(jnp.asarray(valid)[None, None], t, NEG)
    t = t.reshape(N_HEADS // 2, 2, 2 * NA_ROWS - 1, GRID_W, GRID_W)
    return t.transpose(0, 2, 3, 1, 4).reshape(N_HEADS // 2, 2 * NA_ROWS - 1, GRID_W, 2 * GRID_W)


def _merge_kernel(x_ref, g_ref, ya_ref, yb_ref, yc_ref, wg_ref, wb_ref, wo_ref, o_ref):
    x = x_ref[0]
    h = _rms(x, g_ref[...]).astype(BF16)
    merged = None
    for i, y_ref in enumerate((ya_ref, yb_ref, yc_ref)):
        gate = _sigmoid(_dot(h, wg_ref[:, i * D_MODEL:(i + 1) * D_MODEL]))
        t = gate * _dot(y_ref[0], wb_ref[i])
        merged = t if merged is None else merged + t
    o_ref[0] = x + _dot(merged.astype(BF16), wo_ref[...])


def _merge(x, g, ya, yb, yc, wg, wb, wo, tm):
    B, L, _ = x.shape
    row = lambda b, i: (b, i, 0)
    y_spec = pl.BlockSpec((1, tm, 512), row)
    return pl.pallas_call(
        _merge_kernel,
        grid=(B, L // tm),
        in_specs=[pl.BlockSpec((1, tm, D_MODEL), row), _resident((1, D_MODEL)), y_spec, y_spec, y_spec,
                  _resident(wg.shape), _resident(wb.shape), _resident(wo.shape)],
        out_specs=pl.BlockSpec((1, tm, D_MODEL), row),
        out_shape=jax.ShapeDtypeStruct(x.shape, F32),
        compiler_params=_params(2), name="merge_out",
    )(x, g, ya, yb, yc, wg, wb, wo)


def _memkv_kernel(m_ref, g_ref, w_ref, o_ref):
    h = _rms(m_ref[0], g_ref[...]).astype(BF16)
    for c in range(0, 2 * D_MODEL, 2 * MXU_N):
        o_ref[0, :, c:c + 2 * MXU_N] = _dot(h, w_ref[:, c:c + 2 * MXU_N]).astype(o_ref.dtype)


def _memkv(mem, g, w):
    B, M, _ = mem.shape
    return pl.pallas_call(
        _memkv_kernel,
        grid=(B,),
        in_specs=[pl.BlockSpec((1, M, D_MODEL), lambda b: (b, 0, 0)), _resident((1, D_MODEL)),
                  _resident(w.shape)],
        out_specs=pl.BlockSpec((1, M, 2 * D_MODEL), lambda b: (b, 0, 0)),
        out_shape=jax.ShapeDtypeStruct((B, M, 2 * D_MODEL), BF16),
        compiler_params=_params(1), name="mem_kv",
    )(mem, g, w)


def _xattn_kernel(x_ref, g_ref, wq_ref, kv_ref, wo_ref, o_ref):
    x = x_ref[0]
    q = _dot(_rms(x, g_ref[...]).astype(BF16), wq_ref[...]).astype(BF16)
    outs = []
    for h in range(X_HEADS):
        lo = h * X_HEAD_DIM
        kh = kv_ref[0, :, lo:lo + X_HEAD_DIM]
        vh = kv_ref[0, :, D_MODEL + lo:D_MODEL + lo + X_HEAD_DIM]
        s = _dot_nt(q[:, lo:lo + X_HEAD_DIM], kh) * (X_HEAD_DIM ** -0.5)
        p = jnp.exp(s - jnp.max(s, axis=-1, keepdims=True))
        den = jnp.sum(p, axis=-1, keepdims=True)
        outs.append((_dot(p.astype(BF16), vh) / den).astype(BF16))
    o_ref[0] = x + _dot(jnp.concatenate(outs, axis=1), wo_ref[...])


def _xattn(x, g, wq, kv, wo, tm):
    B, L, _ = x.shape
    row = lambda b, i: (b, i, 0)
    return pl.pallas_call(
        _xattn_kernel,
        grid=(B, L // tm),
        in_specs=[pl.BlockSpec((1, tm, D_MODEL), row), _resident((1, D_MODEL)), _resident(wq.shape),
                  pl.BlockSpec((1, MEM_LEN, 2 * D_MODEL), lambda b, i: (b, 0, 0)),
                  _resident(wo.shape)],
        out_specs=pl.BlockSpec((1, tm, D_MODEL), row),
        out_shape=jax.ShapeDtypeStruct(x.shape, F32),
        compiler_params=_params(2), name="mem_xattn",
    )(x, g, wq, kv, wo)


def _ffn_kernel(x_ref, g_ref, wgu_ref, wd_ref, gf_ref, o_ref, act_ref, *, final_norm):
    x = x_ref[0]
    h = _rms(x, g_ref[...]).astype(BF16)
    for c in range(0, D_FF, MXU_N):
        gate = _dot(h, wgu_ref[:, c:c + MXU_N])
        up = _dot(h, wgu_ref[:, D_FF + c:D_FF + c + MXU_N])
        act_ref[:, c:c + MXU_N] = (gate * _sigmoid(gate) * up).astype(BF16)
    y = x + _dot(act_ref[...], wd_ref[...])
    if final_norm:
        y = _rms(y, gf_ref[...])
    o_ref[0] = y


def _ffn(x, g, wgu, wd, g_final, final_norm, tm):
    B, L, _ = x.shape
    row = lambda b, i: (b, i, 0)
    return pl.pallas_call(
        functools.partial(_ffn_kernel, final_norm=final_norm),
        grid=(B, L // tm),
        in_specs=[pl.BlockSpec((1, tm, D_MODEL), row), _resident((1, D_MODEL)),
                  _resident(wgu.shape), _resident(wd.shape), _resident((1, D_MODEL))],
        out_specs=pl.BlockSpec((1, tm, D_MODEL), row),
        out_shape=jax.ShapeDtypeStruct(x.shape, F32),
        scratch_shapes=[pltpu.VMEM((tm, D_FF), BF16)],
        compiler_params=_params(2), name="swiglu",
    )(x, g, wgu, wd, g_final)


TM = 512


def _trunk(x, mem, layers, g_final):
    for l, p in enumerate(layers):
        pa, pr, pn = _proj(x, p["g_mix"], p["w_in"], TM)
        ya = _wattn(pa, p["a_bias"], p["a_sink"])
        yb = _retention(pr, p["r_lg"], p["r_gchunk"])
        yc = _nattn(pn, p["n_bias"])
        x = _merge(x, p["g_mix"], ya, yb, yc, p["w_gate"], p["w_branch"], p["w_mix_out"], TM)
        kv = _memkv(mem, p["g_mem"], p["w_xkv"])
        x = _xattn(x, p["g_xattn"], p["w_xq"], kv, p["w_xo"], TM)
        x = _ffn(x, p["g_ffn"], p["w_gate_up"], p["w_down"], g_final, l == DEPTH - 1, TM)
    return x


def kernel(x_prompt, x_sample, mem_prompt, mem_sample, g_mix, w_in, attn_sink, ret_decay, na_rpb,
           w_branch, w_mix_out, g_xattn, g_mem, w_xq, w_xkv, w_xo, g_ffn, w_gate_up, w_down, g_final):
    layers = []
    for l in range(DEPTH):
        a_bias, a_sink = _wattn_tables(attn_sink[l])
        lg = jax.nn.log_sigmoid(ret_decay[l].astype(F32))
        layers.append(dict(
            g_mix=g_mix[l].reshape(1, D_MODEL),
            w_in=w_in[l, :, :W_MIX].astype(BF16), w_gate=w_in[l, :, W_MIX:].astype(BF16),
            a_bias=a_bias, a_sink=a_sink,
            r_lg=lg, r_gchunk=jnp.exp(R_CHUNK * lg),
            n_bias=_na_bias(na_rpb[l]),
            w_branch=w_branch[l].astype(BF16), w_mix_out=w_mix_out[l].astype(BF16),
            g_xattn=g_xattn[l].reshape(1, D_MODEL), g_mem=g_mem[l].reshape(1, D_MODEL),
            w_xq=w_xq[l].astype(BF16), w_xkv=w_xkv[l].astype(BF16), w_xo=w_xo[l].astype(BF16),
            g_ffn=g_ffn[l].reshape(1, D_MODEL),
            w_gate_up=w_gate_up[l].astype(BF16), w_down=w_down[l].astype(BF16)))
    gf = g_final.reshape(1, D_MODEL)
    return (_trunk(x_prompt, mem_prompt, layers, gf), _trunk(x_sample, mem_sample, layers, gf))
```

```python
import functools

import numpy as np
import jax
import jax.numpy as jnp
from jax import lax
from jax.experimental import pallas as pl
from jax.experimental.pallas import tpu as pltpu

F32 = jnp.float32
BF16 = jnp.bfloat16

D_MODEL = 1024
DEPTH = 2
EPS = 1e-6
HEAD_DIM = 64
A_HEADS = 8
A_KV_HEADS = 2
A_GROUPS = A_HEADS // A_KV_HEADS
A_WINDOW = 128
A_BLOCK = 128
R_HEADS = 4
R_DK = 128
R_DV = 128
R_CHUNK = 128
N_HEADS = 8
GRID_W = 64
NA_ROWS = 8
NA_COLS = 16
MEM_LEN = 256
X_HEADS = 4
X_HEAD_DIM = D_MODEL // X_HEADS
D_FF = 2816
A_Q = A_HEADS * HEAD_DIM
A_KV = A_KV_HEADS * HEAD_DIM
R_W = R_HEADS * R_DK
N_W = N_HEADS * HEAD_DIM
W_A = A_Q + 2 * A_KV
W_R = 4 * R_W
W_N = 3 * N_W
W_MIX = W_A + W_R + W_N
W_G = 3 * D_MODEL
NEG = -1e30

LANE = 128
MXU_N = 256
VMEM_LIMIT = 56 * 1024 * 1024


def _params(n_axes, flags=None):
    return pltpu.CompilerParams(
        dimension_semantics=("arbitrary",) * n_axes, vmem_limit_bytes=VMEM_LIMIT, flags=flags)


def _resident(shape):
    return pl.BlockSpec(shape, lambda *_: (0,) * len(shape), pipeline_mode=pl.Buffered(1))


def _rms(x, g):
    return x * lax.rsqrt(jnp.mean(x * x, axis=-1, keepdims=True) + EPS) * g


def _sigmoid(x):
    return 1.0 / (1.0 + jnp.exp(-x))


def _dot(a, b):
    return jnp.dot(a, b, preferred_element_type=F32)


def _dot_nt(a, b):
    return lax.dot_general(a, b, (((1,), (1,)), ((), ())), preferred_element_type=F32)


def _dot_tn(a, b):
    return lax.dot_general(a, b, (((0,), (0,)), ((), ())), preferred_element_type=F32)


def _proj_kernel(x_ref, g_ref, w_ref, oa_ref, or_ref, on_ref):
    h = _rms(x_ref[0], g_ref[...]).astype(BF16)
    col = 0
    for o_ref, width in ((oa_ref, W_A), (or_ref, W_R), (on_ref, W_N)):
        for c in range(0, width, 2 * MXU_N):
            n = min(2 * MXU_N, width - c)
            o_ref[0, :, c:c + n] = _dot(h, w_ref[:, col + c:col + c + n]).astype(o_ref.dtype)
        col += width


def _proj(x, g, w, tm):
    B, L, _ = x.shape
    row = lambda b, i: (b, i, 0)
    return pl.pallas_call(
        _proj_kernel,
        grid=(B, L // tm),
        in_specs=[pl.BlockSpec((1, tm, D_MODEL), row), _resident((1, D_MODEL)),
                  _resident((D_MODEL, W_MIX))],
        out_specs=[pl.BlockSpec((1, tm, W_A), row), pl.BlockSpec((1, tm, W_R), row),
                   pl.BlockSpec((1, tm, W_N), row)],
        out_shape=[jax.ShapeDtypeStruct((B, L, W_A), BF16), jax.ShapeDtypeStruct((B, L, W_R), F32),
                   jax.ShapeDtypeStruct((B, L, W_N), BF16)],
        compiler_params=_params(2), name="proj_in",
    )(x, g, w)


WA_QB = 4
WA_STAGE = 4
WA_TAIL = 16


def _wattn_kernel(q_ref, kp_ref, kc_ref, kn_ref, vp_ref, vc_ref, vn_ref, bias_ref, sink_ref, ones_ref,
                  o_ref, *, seq_len):
    n = pl.program_id(1)
    win = 3 * A_BLOCK
    k = jnp.concatenate([kp_ref[0], kc_ref[0], kn_ref[0]], axis=0).astype(F32) * (HEAD_DIM ** -0.5)
    v = jnp.concatenate([vp_ref[0], vc_ref[0], vn_ref[0]], axis=0).astype(F32)
    lo = lax.broadcasted_iota(jnp.int32, k.shape, 1) < HEAD_DIM
    swap = lambda x: jnp.concatenate([x[:, HEAD_DIM:], x[:, :HEAD_DIM]], axis=1)
    ks, vs = swap(k), swap(v)
    k_lo = [jnp.where(lo, k, 0.0).astype(BF16), jnp.where(lo, ks, 0.0).astype(BF16)]
    k_hi = [jnp.where(lo, 0.0, ks).astype(BF16), jnp.where(lo, 0.0, k).astype(BF16)]
    v_lo = [jnp.where(lo, v, 0.0).astype(BF16), jnp.where(lo, vs, 0.0).astype(BF16)]
    v_hi = [jnp.where(lo, 0.0, vs).astype(BF16), jnp.where(lo, 0.0, v).astype(BF16)]
    j = lax.broadcasted_iota(jnp.int32, (win, 1), 0)
    units = [(i, h) for i in range(WA_QB) for h in range(A_KV_HEADS)]
    wrows = lambda i: slice(i * A_BLOCK, i * A_BLOCK + win)

    def scores(i, h):
        rows = slice(i * A_BLOCK, (i + 1) * A_BLOCK)
        c0 = h * A_GROUPS * HEAD_DIM
        qh = jnp.concatenate([q_ref[0, rows, c0:c0 + 2 * HEAD_DIM],
                              q_ref[0, rows, c0 + 2 * HEAD_DIM:c0 + 4 * HEAD_DIM]], axis=0)
        return [_dot_nt(kw[wrows(i)], qh) for kw in (k_lo[h], k_hi[h])]

    def softmax(i, h, par, s):
        s = s + bias_ref[h, par]
        if i in (0, WA_QB - 1):
            kpos = (n * WA_QB + i - 1) * A_BLOCK + j
            s = jnp.where((kpos >= 0) & (kpos < seq_len), s, NEG)
        sk = sink_ref[h, par]
        m = jnp.maximum(jnp.max(s, axis=0, keepdims=True), sk)
        return jnp.exp(s - m).astype(BF16), jnp.exp(sk - m)

    for g0 in range(0, len(units), WA_STAGE):
        grp = units[g0:g0 + WA_STAGE]
        all_s = [scores(i, h) for i, h in grp]
        all_p = []
        for (i, h), ss in zip(grp, all_s):
            (pe, ee), (po, eo) = [softmax(i, h, par, s) for par, s in enumerate(ss)]
            tail = jnp.concatenate([ee, eo, jnp.zeros((WA_TAIL - 2, 2 * A_BLOCK), F32)], axis=0)
            all_p.append(jnp.concatenate([pe, po, tail.astype(BF16)], axis=0))
        vtail = jnp.zeros((WA_TAIL, A_KV), BF16)
        all_o = [_dot_tn(p, jnp.concatenate(
                     [jnp.concatenate([v_lo[h][wrows(i)], v_hi[h][wrows(i)], vtail], axis=0),
                      ones_ref[...]], axis=1))
                 for (i, h), p in zip(grp, all_p)]
        for (i, h), res in zip(grp, all_o):
            res = res[:, :A_KV] * (1.0 / res[:, A_KV:])
            c0 = h * A_GROUPS * HEAD_DIM
            o_ref[0, i * A_BLOCK:(i + 1) * A_BLOCK, c0:c0 + 4 * HEAD_DIM] = jnp.concatenate(
                [res[:A_BLOCK], res[A_BLOCK:]], axis=1).astype(o_ref.dtype)


def _wattn(pa, bias, sink):
    B, L, _ = pa.shape
    ones = np.zeros((6 * A_BLOCK + WA_TAIL, A_KV), np.float32)
    ones[:3 * A_BLOCK, :HEAD_DIM] = 1.0
    ones[3 * A_BLOCK:6 * A_BLOCK, HEAD_DIM:] = 1.0
    ones[6 * A_BLOCK, :HEAD_DIM] = 1.0
    ones[6 * A_BLOCK + 1, HEAD_DIM:] = 1.0
    ones = jnp.asarray(ones, BF16)
    nb = L // A_BLOCK
    assert nb % WA_QB == 0
    kcol, vcol = A_Q // A_KV, A_Q // A_KV + 1
    prev = lambda c: pl.BlockSpec((1, A_BLOCK, A_KV), lambda b, n: (b, jnp.maximum(n * WA_QB - 1, 0), c))
    cur = lambda c: pl.BlockSpec((1, WA_QB * A_BLOCK, A_KV), lambda b, n: (b, n, c))
    nxt = lambda c: pl.BlockSpec((1, A_BLOCK, A_KV),
                                 lambda b, n: (b, jnp.minimum((n + 1) * WA_QB, nb - 1), c))
    return pl.pallas_call(
        functools.partial(_wattn_kernel, seq_len=L),
        grid=(B, nb // WA_QB),
        in_specs=[pl.BlockSpec((1, WA_QB * A_BLOCK, A_Q), lambda b, n: (b, n, 0)),
                  prev(kcol), cur(kcol), nxt(kcol), prev(vcol), cur(vcol), nxt(vcol),
                  _resident(bias.shape), _resident(sink.shape), _resident(ones.shape)],
        out_specs=pl.BlockSpec((1, WA_QB * A_BLOCK, A_Q), lambda b, n: (b, n, 0)),
        out_shape=jax.ShapeDtypeStruct((B, L, A_Q), BF16),
        compiler_params=_params(2), name="window_gqa",
    )(pa, pa, pa, pa, pa, pa, pa, bias, sink, ones)


def _wattn_tables(attn_sink):
    i = np.arange(A_BLOCK)[None, :]
    j = np.arange(3 * A_BLOCK)[:, None]
    dist = np.abs(A_BLOCK + i - j)
    slopes = jnp.exp2(-8.0 * jnp.arange(1, A_HEADS + 1, dtype=F32) / A_HEADS)
    bias = -slopes[:, None, None] * jnp.asarray(dist, F32)[None]
    bias = jnp.where(jnp.asarray(dist <= A_WINDOW)[None], bias, NEG)
    bias = bias.reshape(A_KV_HEADS, 2, 2, 3 * A_BLOCK, A_BLOCK).transpose(0, 2, 3, 1, 4)
    bias = bias.reshape(A_KV_HEADS, 2, 3 * A_BLOCK, 2 * A_BLOCK)
    sink = jnp.repeat(attn_sink.astype(F32), A_BLOCK).reshape(A_KV_HEADS, 2, 2, A_BLOCK)
    sink = sink.transpose(0, 2, 1, 3).reshape(A_KV_HEADS, 2, 1, 2 * A_BLOCK)
    return bias, sink


RET_UNROLL = 8


def _ret_kernel(lg_ref, gc_ref, q_ref, k_ref, v_ref, g_ref, o_ref, st_ref, *, nc):
    h = pl.program_id(1)
    C = R_CHUNK
    lgf, lgb = lg_ref[0, h], lg_ref[1, h]
    gcf, gcb = gc_ref[0, h], gc_ref[1, h]
    diff = (lax.broadcasted_iota(jnp.int32, (C, C), 0)
            - lax.broadcasted_iota(jnp.int32, (C, C), 1)).astype(F32)
    decay = (jnp.where(diff >= 0, jnp.exp(jnp.maximum(diff, 0.0) * lgf), 0.0)
             + jnp.where(diff <= 0, jnp.exp(jnp.maximum(-diff, 0.0) * lgb), 0.0))
    idx = lax.broadcasted_iota(jnp.int32, (C, R_DK), 0).astype(F32)
    wq2 = jnp.concatenate([jnp.exp((idx + 1.0) * lgf), jnp.exp((C - idx) * lgb)], axis=1)
    wk2 = jnp.concatenate([jnp.exp((C - 1.0 - idx) * lgf), jnp.exp(idx * lgb)], axis=1)

    def rows(n):
        return pl.ds(pl.multiple_of(n * C, C), C)

    def kv_body(i, carry):
        for u in range(RET_UNROLL):
            n = i * RET_UNROLL + u
            k = k_ref[0, rows(n), :] * (R_DK ** -0.5)
            kw = (jnp.concatenate([k, k], axis=1) * wk2).astype(BF16)
            st_ref[n] = _dot_tn(kw, v_ref[0, rows(n), :].astype(BF16))
        return carry

    lax.fori_loop(0, nc // RET_UNROLL, kv_body, 0)

    def scan_body(t, carry):
        sf, sb = carry
        nb = nc - 1 - t
        kvf = st_ref[t, :R_DK, :]
        st_ref[t, :R_DK, :] = sf
        kvb = st_ref[nb, R_DK:, :]
        st_ref[nb, R_DK:, :] = sb
        return sf * gcf + kvf, sb * gcb + kvb

    zero = jnp.zeros((R_DK, R_DV), F32)
    lax.fori_loop(0, nc, scan_body, (zero, zero))

    def out_body(i, carry):
        ns = [i * RET_UNROLL + u for u in range(RET_UNROLL)]
        qs = [q_ref[0, rows(n), :] for n in ns]
        inner = [_dot_nt(q.astype(BF16), (k_ref[0, rows(n), :] * (R_DK ** -0.5)).astype(BF16))
                 for n, q in zip(ns, qs)]
        lhs = [jnp.concatenate([s * decay, jnp.concatenate([q, q], axis=1) * wq2], axis=1).astype(BF16)
               for s, q in zip(inner, qs)]
        outs = [_dot(a, jnp.concatenate([v_ref[0, rows(n), :].astype(BF16),
                                         st_ref[n].astype(BF16)], axis=0))
                for n, a in zip(ns, lhs)]
        for n, o in zip(ns, outs):
            o = o * lax.rsqrt(jnp.mean(o * o, axis=-1, keepdims=True) + EPS)
            g = g_ref[0, rows(n), :]
            o_ref[0, rows(n), :] = (g * _sigmoid(g) * o).astype(o_ref.dtype)
        return carry

    lax.fori_loop(0, nc // RET_UNROLL, out_body, 0)


def _retention(pr, lg, gchunk):
    B, L, _ = pr.shape
    nc = L // R_CHUNK
    assert nc % RET_UNROLL == 0
    part = lambda p: pl.BlockSpec((1, L, R_DK), lambda b, h: (b, 0, p * R_HEADS + h))
    smem = pl.BlockSpec(memory_space=pltpu.SMEM)
    return pl.pallas_call(
        functools.partial(_ret_kernel, nc=nc),
        grid=(B, R_HEADS),
        in_specs=[smem, smem, part(0), part(1), part(2), part(3)],
        out_specs=pl.BlockSpec((1, L, R_DV), lambda b, h: (b, 0, h)),
        out_shape=jax.ShapeDtypeStruct((B, L, R_W), BF16),
        scratch_shapes=[pltpu.VMEM((nc, 2 * R_DK, R_DV), F32)],
        compiler_params=_params(2), name="retention",
    )(lg, gchunk, pr, pr, pr, pr)


NA_HG = 4
NA_UNROLL = 8


def _na_kernel(q_ref, k_ref, v_ref, bias_ref, o_ref, *, n_rows):
    keys = NA_ROWS * GRID_W
    head = lax.broadcasted_iota(jnp.int32, (GRID_W, NA_HG * HEAD_DIM), 1) // HEAD_DIM

    def qrows(r):
        return pl.ds(pl.multiple_of(r * GRID_W, GRID_W), GRID_W)

    def krows(r):
        rs = jnp.clip(r - NA_ROWS // 2, 0, n_rows - NA_ROWS)
        return pl.ds(pl.multiple_of(rs * GRID_W, GRID_W), keys)

    def scores(r):
        q = q_ref[0, qrows(r), :].astype(F32) * (HEAD_DIM ** -0.5)
        qbd = jnp.concatenate([jnp.where(head == h, q, 0.0) for h in range(NA_HG)], axis=0).astype(BF16)
        return _dot_nt(k_ref[0, krows(r), :], qbd)

    def softmax(r, s):
        rs = jnp.clip(r - NA_ROWS // 2, 0, n_rows - NA_ROWS)
        s = s + jnp.concatenate(
            [bias_ref[0, rs - r + (NA_ROWS - 1) + j] for j in range(NA_ROWS)], axis=0)
        p = jnp.exp(s - jnp.max(s, axis=0, keepdims=True))
        return (p * (1.0 / jnp.sum(p, axis=0, keepdims=True))).astype(BF16)

    def body(i, carry):
        rs_ = [i * NA_UNROLL + u for u in range(NA_UNROLL)]
        all_s = [scores(r) for r in rs_]
        all_p = [softmax(r, s) for r, s in zip(rs_, all_s)]
        all_o = [_dot_tn(p, v_ref[0, krows(r), :]) for r, p in zip(rs_, all_p)]
        for r, o in zip(rs_, all_o):
            out = o[:GRID_W]
            for h in range(1, NA_HG):
                out = jnp.where(head == h, o[h * GRID_W:(h + 1) * GRID_W], out)
            o_ref[0, qrows(r), :] = out.astype(o_ref.dtype)
        return carry

    lax.fori_loop(0, n_rows // NA_UNROLL, body, 0)


def _nattn(pn, bias):
    B, L, _ = pn.shape
    groups = N_HEADS // NA_HG
    assert (L // GRID_W) % NA_UNROLL == 0
    width = NA_HG * HEAD_DIM
    part = lambda p: pl.BlockSpec((1, L, width), lambda b, g: (b, 0, p * groups + g))
    return pl.pallas_call(
        functools.partial(_na_kernel, n_rows=L // GRID_W),
        grid=(B, groups),
        in_specs=[part(0), part(1), part(2),
                  pl.BlockSpec((1,) + bias.shape[1:], lambda b, g: (g, 0, 0, 0))],
        out_specs=pl.BlockSpec((1, L, width), lambda b, g: (b, 0, g)),
        out_shape=jax.ShapeDtypeStruct((B, L, N_W), BF16),
        compiler_params=_params(2), name="neighbourhood_attn",
    )(pn, pn, pn, bias)


def _na_bias(rpb):
    c = np.arange(GRID_W)
    cs = np.clip(c - NA_COLS // 2, 0, GRID_W - NA_COLS)
    valid = (c[:, None] >= cs[None, :]) & (c[:, None] < cs[None, :] + NA_COLS)
    dc = c[:, None] - c[None, :] + NA_COLS - 1
    onehot = valid[None] & (dc[None] == np.arange(2 * NA_COLS - 1)[:, None, None])
    t = jnp.where(jnp.asarray(onehot)[None, None], rpb.astype(F32)[:, :, :, None, None], 0.0).sum(axis=2)
    t = jnp.where(jnp.asarray(valid)[None, None], t, NEG)
    t = t.reshape(N_HEADS // NA_HG, NA_HG, 2 * NA_ROWS - 1, GRID_W, GRID_W)
    return t.transpose(0, 2, 3, 1, 4).reshape(N_HEADS // NA_HG, 2 * NA_ROWS - 1, GRID_W, NA_HG * GRID_W)


def _merge_kernel(x_ref, g_ref, ya_ref, yb_ref, yc_ref, wg_ref, wb_ref, wo_ref, o_ref):
    x = x_ref[0]
    h = _rms(x, g_ref[...]).astype(BF16)
    merged = None
    for i, y_ref in enumerate((ya_ref, yb_ref, yc_ref)):
        gate = _sigmoid(_dot(h, wg_ref[:, i * D_MODEL:(i + 1) * D_MODEL]))
        t = gate * _dot(y_ref[0], wb_ref[i])
        merged = t if merged is None else merged + t
    o_ref[0] = x + _dot(merged.astype(BF16), wo_ref[...])


def _merge(x, g, ya, yb, yc, wg, wb, wo, tm):
    B, L, _ = x.shape
    row = lambda b, i: (b, i, 0)
    y_spec = pl.BlockSpec((1, tm, 512), row)
    return pl.pallas_call(
        _merge_kernel,
        grid=(B, L // tm),
        in_specs=[pl.BlockSpec((1, tm, D_MODEL), row), _resident((1, D_MODEL)), y_spec, y_spec, y_spec,
                  _resident(wg.shape), _resident(wb.shape), _resident(wo.shape)],
        out_specs=pl.BlockSpec((1, tm, D_MODEL), row),
        out_shape=jax.ShapeDtypeStruct(x.shape, F32),
        compiler_params=_params(2), name="merge_out",
    )(x, g, ya, yb, yc, wg, wb, wo)


def _memkv_kernel(m_ref, g_ref, w_ref, o_ref):
    h = _rms(m_ref[0], g_ref[...]).astype(BF16)
    for c in range(0, 2 * D_MODEL, 2 * MXU_N):
        o_ref[0, :, c:c + 2 * MXU_N] = _dot(h, w_ref[:, c:c + 2 * MXU_N]).astype(o_ref.dtype)


def _memkv(mem, g, w):
    B, M, _ = mem.shape
    return pl.pallas_call(
        _memkv_kernel,
        grid=(B,),
        in_specs=[pl.BlockSpec((1, M, D_MODEL), lambda b: (b, 0, 0)), _resident((1, D_MODEL)),
                  _resident(w.shape)],
        out_specs=pl.BlockSpec((1, M, 2 * D_MODEL), lambda b: (b, 0, 0)),
        out_shape=jax.ShapeDtypeStruct((B, M, 2 * D_MODEL), BF16),
        compiler_params=_params(1), name="mem_kv",
    )(mem, g, w)


def _xattn_kernel(x_ref, g_ref, wq_ref, kv_ref, wo_ref, o_ref):
    x = x_ref[0]
    q = _dot(_rms(x, g_ref[...]).astype(BF16), wq_ref[...]).astype(BF16)
    cols = [slice(h * X_HEAD_DIM, (h + 1) * X_HEAD_DIM) for h in range(X_HEADS)]
    scores = [_dot_nt(q[:, c], kv_ref[0, :, c]) * (X_HEAD_DIM ** -0.5) for c in cols]
    probs = [jnp.exp(s - jnp.max(s, axis=-1, keepdims=True)) for s in scores]
    dens = [jnp.sum(p, axis=-1, keepdims=True) for p in probs]
    outs = [_dot(p.astype(BF16), kv_ref[0, :, D_MODEL + c.start:D_MODEL + c.stop])
            for p, c in zip(probs, cols)]
    attn = jnp.concatenate([(o / d).astype(BF16) for o, d in zip(outs, dens)], axis=1)
    o_ref[0] = x + _dot(attn, wo_ref[...])


def _xattn(x, g, wq, kv, wo, tm):
    B, L, _ = x.shape
    row = lambda b, i: (b, i, 0)
    return pl.pallas_call(
        _xattn_kernel,
        grid=(B, L // tm),
        in_specs=[pl.BlockSpec((1, tm, D_MODEL), row), _resident((1, D_MODEL)), _resident(wq.shape),
                  pl.BlockSpec((1, MEM_LEN, 2 * D_MODEL), lambda b, i: (b, 0, 0)),
                  _resident(wo.shape)],
        out_specs=pl.BlockSpec((1, tm, D_MODEL), row),
        out_shape=jax.ShapeDtypeStruct(x.shape, F32),
        compiler_params=_params(2), name="mem_xattn",
    )(x, g, wq, kv, wo)


def _ffn_kernel(x_ref, g_ref, wgu_ref, wd_ref, gf_ref, o_ref, act_ref, *, final_norm):
    x = x_ref[0]
    h = _rms(x, g_ref[...]).astype(BF16)
    for c in range(0, D_FF, MXU_N):
        gate = _dot(h, wgu_ref[:, c:c + MXU_N])
        up = _dot(h, wgu_ref[:, D_FF + c:D_FF + c + MXU_N])
        act_ref[:, c:c + MXU_N] = (gate * _sigmoid(gate) * up).astype(BF16)
    y = x + _dot(act_ref[...], wd_ref[...])
    if final_norm:
        y = _rms(y, gf_ref[...])
    o_ref[0] = y


def _ffn(x, g, wgu, wd, g_final, final_norm, tm):
    B, L, _ = x.shape
    row = lambda b, i: (b, i, 0)
    return pl.pallas_call(
        functools.partial(_ffn_kernel, final_norm=final_norm),
        grid=(B, L // tm),
        in_specs=[pl.BlockSpec((1, tm, D_MODEL), row), _resident((1, D_MODEL)),
                  _resident(wgu.shape), _resident(wd.shape), _resident((1, D_MODEL))],
        out_specs=pl.BlockSpec((1, tm, D_MODEL), row),
        out_shape=jax.ShapeDtypeStruct(x.shape, F32),
        scratch_shapes=[pltpu.VMEM((tm, D_FF), BF16)],
        compiler_params=_params(2), name="swiglu",
    )(x, g, wgu, wd, g_final)


TM = 1024


def _trunk(x, mem, layers, g_final):
    for l, p in enumerate(layers):
        pa, pr, pn = _proj(x, p["g_mix"], p["w_in"], TM)
        ya = _wattn(pa, p["a_bias"], p["a_sink"])
        yb = _retention(pr, p["r_lg"], p["r_gchunk"])
        yc = _nattn(pn, p["n_bias"])
        x = _merge(x, p["g_mix"], ya, yb, yc, p["w_gate"], p["w_branch"], p["w_mix_out"], TM)
        kv = _memkv(mem, p["g_mem"], p["w_xkv"])
        x = _xattn(x, p["g_xattn"], p["w_xq"], kv, p["w_xo"], TM)
        x = _ffn(x, p["g_ffn"], p["w_gate_up"], p["w_down"], g_final, l == DEPTH - 1, TM)
    return x


def kernel(x_prompt, x_sample, mem_prompt, mem_sample, g_mix, w_in, attn_sink, ret_decay, na_rpb,
           w_branch, w_mix_out, g_xattn, g_mem, w_xq, w_xkv, w_xo, g_ffn, w_gate_up, w_down, g_final):
    layers = []
    for l in range(DEPTH):
        a_bias, a_sink = _wattn_tables(attn_sink[l])
        lg = jax.nn.log_sigmoid(ret_decay[l].astype(F32))
        layers.append(dict(
            g_mix=g_mix[l].reshape(1, D_MODEL),
            w_in=w_in[l, :, :W_MIX].astype(BF16), w_gate=w_in[l, :, W_MIX:].astype(BF16),
            a_bias=a_bias, a_sink=a_sink,
            r_lg=lg, r_gchunk=jnp.exp(R_CHUNK * lg),
            n_bias=_na_bias(na_rpb[l]),
            w_branch=w_branch[l].astype(BF16), w_mix_out=w_mix_out[l].astype(BF16),
            g_xattn=g_xattn[l].reshape(1, D_MODEL), g_mem=g_mem[l].reshape(1, D_MODEL),
            w_xq=w_xq[l].astype(BF16), w_xkv=w_xkv[l].astype(BF16), w_xo=w_xo[l].astype(BF16),
            g_ffn=g_ffn[l].reshape(1, D_MODEL),
            w_gate_up=w_gate_up[l].astype(BF16), w_down=w_down[l].astype(BF16)))
    gf = g_final.reshape(1, D_MODEL)
    return (_trunk(x_prompt, mem_prompt, layers, gf), _trunk(x_sample, mem_sample, layers, gf))
```

```python
import functools

import numpy as np
import jax
import jax.numpy as jnp
from jax import lax
from jax.experimental import pallas as pl
from jax.experimental.pallas import tpu as pltpu

F32 = jnp.float32
BF16 = jnp.bfloat16

D_MODEL = 1024
DEPTH = 2
EPS = 1e-6
HEAD_DIM = 64
A_HEADS = 8
A_KV_HEADS = 2
A_GROUPS = A_HEADS // A_KV_HEADS
A_WINDOW = 128
A_BLOCK = 128
R_HEADS = 4
R_DK = 128
R_DV = 128
R_CHUNK = 128
N_HEADS = 8
GRID_W = 64
NA_ROWS = 8
NA_COLS = 16
MEM_LEN = 256
X_HEADS = 4
X_HEAD_DIM = D_MODEL // X_HEADS
D_FF = 2816
A_Q = A_HEADS * HEAD_DIM
A_KV = A_KV_HEADS * HEAD_DIM
R_W = R_HEADS * R_DK
N_W = N_HEADS * HEAD_DIM
W_A = A_Q + 2 * A_KV
W_R = 4 * R_W
W_N = 3 * N_W
W_MIX = W_A + W_R + W_N
W_G = 3 * D_MODEL
NEG = -1e30

LANE = 128
MXU_N = 256
VMEM_LIMIT = 56 * 1024 * 1024


def _params(n_axes, flags=None):
    return pltpu.CompilerParams(
        dimension_semantics=("arbitrary",) * n_axes, vmem_limit_bytes=VMEM_LIMIT, flags=flags)


def _resident(shape):
    return pl.BlockSpec(shape, lambda *_: (0,) * len(shape), pipeline_mode=pl.Buffered(1))


def _rms(x, g):
    return x * lax.rsqrt(jnp.mean(x * x, axis=-1, keepdims=True) + EPS) * g


def _sigmoid(x):
    return 1.0 / (1.0 + jnp.exp(-x))


def _dot(a, b):
    return jnp.dot(a, b, preferred_element_type=F32)


def _dot_nt(a, b):
    return lax.dot_general(a, b, (((1,), (1,)), ((), ())), preferred_element_type=F32)


def _dot_tn(a, b):
    return lax.dot_general(a, b, (((0,), (0,)), ((), ())), preferred_element_type=F32)


def _proj_kernel(x_ref, g_ref, w_ref, oa_ref, or_ref, on_ref):
    h = _rms(x_ref[0], g_ref[...]).astype(BF16)
    col = 0
    for o_ref, width in ((oa_ref, W_A), (or_ref, W_R), (on_ref, W_N)):
        for c in range(0, width, 2 * MXU_N):
            n = min(2 * MXU_N, width - c)
            o_ref[0, :, c:c + n] = _dot(h, w_ref[:, col + c:col + c + n]).astype(o_ref.dtype)
        col += width


def _proj(x, g, w, tm):
    B, L, _ = x.shape
    row = lambda b, i: (b, i, 0)
    return pl.pallas_call(
        _proj_kernel,
        grid=(B, L // tm),
        in_specs=[pl.BlockSpec((1, tm, D_MODEL), row), _resident((1, D_MODEL)),
                  _resident((D_MODEL, W_MIX))],
        out_specs=[pl.BlockSpec((1, tm, W_A), row), pl.BlockSpec((1, tm, W_R), row),
                   pl.BlockSpec((1, tm, W_N), row)],
        out_shape=[jax.ShapeDtypeStruct((B, L, W_A), BF16), jax.ShapeDtypeStruct((B, L, W_R), F32),
                   jax.ShapeDtypeStruct((B, L, W_N), BF16)],
        compiler_params=_params(2), name="proj_in",
    )(x, g, w)


WA_QB = 8
WA_STAGE = 4
WA_TAIL = 16


def _wattn_kernel(q_ref, kp_ref, kc_ref, kn_ref, vp_ref, vc_ref, vn_ref, bias_ref, sink_ref, ones_ref,
                  o_ref, *, seq_len):
    n = pl.program_id(1)
    win = 3 * A_BLOCK
    k = jnp.concatenate([kp_ref[0], kc_ref[0], kn_ref[0]], axis=0).astype(F32) * (HEAD_DIM ** -0.5)
    v = jnp.concatenate([vp_ref[0], vc_ref[0], vn_ref[0]], axis=0).astype(F32)
    lo = lax.broadcasted_iota(jnp.int32, k.shape, 1) < HEAD_DIM
    swap = lambda x: jnp.concatenate([x[:, HEAD_DIM:], x[:, :HEAD_DIM]], axis=1)
    ks, vs = swap(k), swap(v)
    k_lo = [jnp.where(lo, k, 0.0).astype(BF16), jnp.where(lo, ks, 0.0).astype(BF16)]
    k_hi = [jnp.where(lo, 0.0, ks).astype(BF16), jnp.where(lo, 0.0, k).astype(BF16)]
    v_lo = [jnp.where(lo, v, 0.0).astype(BF16), jnp.where(lo, vs, 0.0).astype(BF16)]
    v_hi = [jnp.where(lo, 0.0, vs).astype(BF16), jnp.where(lo, 0.0, v).astype(BF16)]
    j = lax.broadcasted_iota(jnp.int32, (win, 1), 0)
    units = [(i, h) for i in range(WA_QB) for h in range(A_KV_HEADS)]
    wrows = lambda i: slice(i * A_BLOCK, i * A_BLOCK + win)

    def scores(i, h):
        rows = slice(i * A_BLOCK, (i + 1) * A_BLOCK)
        c0 = h * A_GROUPS * HEAD_DIM
        qh = jnp.concatenate([q_ref[0, rows, c0:c0 + 2 * HEAD_DIM],
                              q_ref[0, rows, c0 + 2 * HEAD_DIM:c0 + 4 * HEAD_DIM]], axis=0)
        return [_dot_nt(kw[wrows(i)], qh) for kw in (k_lo[h], k_hi[h])]

    def softmax(i, h, par, s):
        s = s + bias_ref[h, par]
        if i in (0, WA_QB - 1):
            kpos = (n * WA_QB + i - 1) * A_BLOCK + j
            s = jnp.where((kpos >= 0) & (kpos < seq_len), s, NEG)
        sk = sink_ref[h, par]
        m = jnp.maximum(jnp.max(s, axis=0, keepdims=True), sk)
        return jnp.exp(s - m).astype(BF16), jnp.exp(sk - m)

    for g0 in range(0, len(units), WA_STAGE):
        grp = units[g0:g0 + WA_STAGE]
        all_s = [scores(i, h) for i, h in grp]
        all_p = []
        for (i, h), ss in zip(grp, all_s):
            (pe, ee), (po, eo) = [softmax(i, h, par, s) for par, s in enumerate(ss)]
            tail = jnp.concatenate([ee, eo, jnp.zeros((WA_TAIL - 2, 2 * A_BLOCK), F32)], axis=0)
            all_p.append(jnp.concatenate([pe, po, tail.astype(BF16)], axis=0))
        vtail = jnp.zeros((WA_TAIL, A_KV), BF16)
        all_o = [_dot_tn(p, jnp.concatenate(
                     [jnp.concatenate([v_lo[h][wrows(i)], v_hi[h][wrows(i)], vtail], axis=0),
                      ones_ref[...]], axis=1))
                 for (i, h), p in zip(grp, all_p)]
        for (i, h), res in zip(grp, all_o):
            res = res[:, :A_KV] * (1.0 / res[:, A_KV:])
            c0 = h * A_GROUPS * HEAD_DIM
            o_ref[0, i * A_BLOCK:(i + 1) * A_BLOCK, c0:c0 + 4 * HEAD_DIM] = jnp.concatenate(
                [res[:A_BLOCK], res[A_BLOCK:]], axis=1).astype(o_ref.dtype)


def _wattn(pa, bias, sink):
    B, L, _ = pa.shape
    ones = np.zeros((6 * A_BLOCK + WA_TAIL, A_KV), np.float32)
    ones[:3 * A_BLOCK, :HEAD_DIM] = 1.0
    ones[3 * A_BLOCK:6 * A_BLOCK, HEAD_DIM:] = 1.0
    ones[6 * A_BLOCK, :HEAD_DIM] = 1.0
    ones[6 * A_BLOCK + 1, HEAD_DIM:] = 1.0
    ones = jnp.asarray(ones, BF16)
    nb = L // A_BLOCK
    assert nb % WA_QB == 0
    kcol, vcol = A_Q // A_KV, A_Q // A_KV + 1
    prev = lambda c: pl.BlockSpec((1, A_BLOCK, A_KV), lambda b, n: (b, jnp.maximum(n * WA_QB - 1, 0), c))
    cur = lambda c: pl.BlockSpec((1, WA_QB * A_BLOCK, A_KV), lambda b, n: (b, n, c))
    nxt = lambda c: pl.BlockSpec((1, A_BLOCK, A_KV),
                                 lambda b, n: (b, jnp.minimum((n + 1) * WA_QB, nb - 1), c))
    return pl.pallas_call(
        functools.partial(_wattn_kernel, seq_len=L),
        grid=(B, nb // WA_QB),
        in_specs=[pl.BlockSpec((1, WA_QB * A_BLOCK, A_Q), lambda b, n: (b, n, 0)),
                  prev(kcol), cur(kcol), nxt(kcol), prev(vcol), cur(vcol), nxt(vcol),
                  _resident(bias.shape), _resident(sink.shape), _resident(ones.shape)],
        out_specs=pl.BlockSpec((1, WA_QB * A_BLOCK, A_Q), lambda b, n: (b, n, 0)),
        out_shape=jax.ShapeDtypeStruct((B, L, A_Q), BF16),
        compiler_params=_params(2), name="window_gqa",
    )(pa, pa, pa, pa, pa, pa, pa, bias, sink, ones)


def _wattn_tables(attn_sink):
    i = np.arange(A_BLOCK)[None, :]
    j = np.arange(3 * A_BLOCK)[:, None]
    dist = np.abs(A_BLOCK + i - j)
    slopes = jnp.exp2(-8.0 * jnp.arange(1, A_HEADS + 1, dtype=F32) / A_HEADS)
    bias = -slopes[:, None, None] * jnp.asarray(dist, F32)[None]
    bias = jnp.where(jnp.asarray(dist <= A_WINDOW)[None], bias, NEG)
    bias = bias.reshape(A_KV_HEADS, 2, 2, 3 * A_BLOCK, A_BLOCK).transpose(0, 2, 3, 1, 4)
    bias = bias.reshape(A_KV_HEADS, 2, 3 * A_BLOCK, 2 * A_BLOCK)
    sink = jnp.repeat(attn_sink.astype(F32), A_BLOCK).reshape(A_KV_HEADS, 2, 2, A_BLOCK)
    sink = sink.transpose(0, 2, 1, 3).reshape(A_KV_HEADS, 2, 1, 2 * A_BLOCK)
    return bias, sink


RET_UNROLL = 16


def _ret_kernel(lg_ref, gc_ref, q_ref, k_ref, v_ref, g_ref, o_ref, st_ref, *, nc):
    h = pl.program_id(1)
    C = R_CHUNK
    lgf, lgb = lg_ref[0, h], lg_ref[1, h]
    gcf, gcb = gc_ref[0, h], gc_ref[1, h]
    diff = (lax.broadcasted_iota(jnp.int32, (C, C), 0)
            - lax.broadcasted_iota(jnp.int32, (C, C), 1)).astype(F32)
    decay = (jnp.where(diff >= 0, jnp.exp(jnp.maximum(diff, 0.0) * lgf), 0.0)
             + jnp.where(diff <= 0, jnp.exp(jnp.maximum(-diff, 0.0) * lgb), 0.0))
    idx = lax.broadcasted_iota(jnp.int32, (C, R_DK), 0).astype(F32)
    wq2 = jnp.concatenate([jnp.exp((idx + 1.0) * lgf), jnp.exp((C - idx) * lgb)], axis=1)
    wk2 = jnp.concatenate([jnp.exp((C - 1.0 - idx) * lgf), jnp.exp(idx * lgb)], axis=1)

    def rows(n):
        return pl.ds(pl.multiple_of(n * C, C), C)

    def kv_body(i, carry):
        for u in range(RET_UNROLL):
            n = i * RET_UNROLL + u
            k = k_ref[0, rows(n), :] * (R_DK ** -0.5)
            kw = (jnp.concatenate([k, k], axis=1) * wk2).astype(BF16)
            st_ref[n] = _dot_tn(kw, v_ref[0, rows(n), :].astype(BF16))
        return carry

    lax.fori_loop(0, nc // RET_UNROLL, kv_body, 0)

    def scan_body(t, carry):
        sf, sb = carry
        nb = nc - 1 - t
        kvf = st_ref[t, :R_DK, :]
        st_ref[t, :R_DK, :] = sf
        kvb = st_ref[nb, R_DK:, :]
        st_ref[nb, R_DK:, :] = sb
        return sf * gcf + kvf, sb * gcb + kvb

    zero = jnp.zeros((R_DK, R_DV), F32)
    lax.fori_loop(0, nc, scan_body, (zero, zero))

    def out_body(i, carry):
        ns = [i * RET_UNROLL + u for u in range(RET_UNROLL)]
        qs = [q_ref[0, rows(n), :] for n in ns]
        inner = [_dot_nt(q.astype(BF16), (k_ref[0, rows(n), :] * (R_DK ** -0.5)).astype(BF16))
                 for n, q in zip(ns, qs)]
        lhs = [jnp.concatenate([s * decay, jnp.concatenate([q, q], axis=1) * wq2], axis=1).astype(BF16)
               for s, q in zip(inner, qs)]
        outs = [_dot(a, jnp.concatenate([v_ref[0, rows(n), :].astype(BF16),
                                         st_ref[n].astype(BF16)], axis=0))
                for n, a in zip(ns, lhs)]
        for n, o in zip(ns, outs):
            o = o * lax.rsqrt(jnp.mean(o * o, axis=-1, keepdims=True) + EPS)
            g = g_ref[0, rows(n), :]
            o_ref[0, rows(n), :] = (g * _sigmoid(g) * o).astype(o_ref.dtype)
        return carry

    lax.fori_loop(0, nc // RET_UNROLL, out_body, 0)


def _retention(pr, lg, gchunk):
    B, L, _ = pr.shape
    nc = L // R_CHUNK
    assert nc % RET_UNROLL == 0
    part = lambda p: pl.BlockSpec((1, L, R_DK), lambda b, h: (b, 0, p * R_HEADS + h))
    smem = pl.BlockSpec(memory_space=pltpu.SMEM)
    return pl.pallas_call(
        functools.partial(_ret_kernel, nc=nc),
        grid=(B, R_HEADS),
        in_specs=[smem, smem, part(0), part(1), part(2), part(3)],
        out_specs=pl.BlockSpec((1, L, R_DV), lambda b, h: (b, 0, h)),
        out_shape=jax.ShapeDtypeStruct((B, L, R_W), BF16),
        scratch_shapes=[pltpu.VMEM((nc, 2 * R_DK, R_DV), F32)],
        compiler_params=_params(2), name="retention",
    )(lg, gchunk, pr, pr, pr, pr)


NA_HG = 4
NA_UNROLL = 16


def _na_kernel(q_ref, k_ref, v_ref, bias_ref, o_ref, *, n_rows):
    keys = NA_ROWS * GRID_W
    head = lax.broadcasted_iota(jnp.int32, (GRID_W, NA_HG * HEAD_DIM), 1) // HEAD_DIM

    def qrows(r):
        return pl.ds(pl.multiple_of(r * GRID_W, GRID_W), GRID_W)

    def krows(r):
        rs = jnp.clip(r - NA_ROWS // 2, 0, n_rows - NA_ROWS)
        return pl.ds(pl.multiple_of(rs * GRID_W, GRID_W), keys)

    def scores(r):
        q = q_ref[0, qrows(r), :].astype(F32) * (HEAD_DIM ** -0.5)
        qbd = jnp.concatenate([jnp.where(head == h, q, 0.0) for h in range(NA_HG)], axis=0).astype(BF16)
        return _dot_nt(k_ref[0, krows(r), :], qbd)

    def softmax(r, s):
        rs = jnp.clip(r - NA_ROWS // 2, 0, n_rows - NA_ROWS)
        s = s + jnp.concatenate(
            [bias_ref[0, rs - r + (NA_ROWS - 1) + j] for j in range(NA_ROWS)], axis=0)
        p = jnp.exp(s - jnp.max(s, axis=0, keepdims=True))
        return (p * (1.0 / jnp.sum(p, axis=0, keepdims=True))).astype(BF16)

    def body(i, carry):
        rs_ = [i * NA_UNROLL + u for u in range(NA_UNROLL)]
        all_s = [scores(r) for r in rs_]
        all_p = [softmax(r, s) for r, s in zip(rs_, all_s)]
        all_o = [_dot_tn(p, v_ref[0, krows(r), :]) for r, p in zip(rs_, all_p)]
        for r, o in zip(rs_, all_o):
            out = o[:GRID_W]
            for h in range(1, NA_HG):
                out = jnp.where(head == h, o[h * GRID_W:(h + 1) * GRID_W], out)
            o_ref[0, qrows(r), :] = out.astype(o_ref.dtype)
        return carry

    lax.fori_loop(0, n_rows // NA_UNROLL, body, 0)


def _nattn(pn, bias):
    B, L, _ = pn.shape
    groups = N_HEADS // NA_HG
    assert (L // GRID_W) % NA_UNROLL == 0
    width = NA_HG * HEAD_DIM
    part = lambda p: pl.BlockSpec((1, L, width), lambda b, g: (b, 0, p * groups + g))
    return pl.pallas_call(
        functools.partial(_na_kernel, n_rows=L // GRID_W),
        grid=(B, groups),
        in_specs=[part(0), part(1), part(2),
                  pl.BlockSpec((1,) + bias.shape[1:], lambda b, g: (g, 0, 0, 0))],
        out_specs=pl.BlockSpec((1, L, width), lambda b, g: (b, 0, g)),
        out_shape=jax.ShapeDtypeStruct((B, L, N_W), BF16),
        compiler_params=_params(2), name="neighbourhood_attn",
    )(pn, pn, pn, bias)


def _na_bias(rpb):
    c = np.arange(GRID_W)
    cs = np.clip(c - NA_COLS // 2, 0, GRID_W - NA_COLS)
    valid = (c[:, None] >= cs[None, :]) & (c[:, None] < cs[None, :] + NA_COLS)
    a = rpb.astype(F32)
    period = 2 * GRID_W
    z = jnp.concatenate([a[..., NA_COLS - 1::-1],
                         jnp.full(a.shape[:-1] + (period - (2 * NA_COLS - 1),), NEG, F32),
                         a[..., :NA_COLS - 1:-1]], axis=-1)
    t = jnp.tile(z, GRID_W)[..., :GRID_W * (period - 1)]
    t = t.reshape(a.shape[:-1] + (GRID_W, period - 1))[..., :GRID_W]
    t = jnp.where(jnp.asarray(valid)[None, None], t, NEG)
    t = t.reshape(N_HEADS // NA_HG, NA_HG, 2 * NA_ROWS - 1, GRID_W, GRID_W)
    return t.transpose(0, 2, 3, 1, 4).reshape(N_HEADS // NA_HG, 2 * NA_ROWS - 1, GRID_W, NA_HG * GRID_W)


def _merge_kernel(x_ref, g_ref, ya_ref, yb_ref, yc_ref, wg_ref, wb_ref, wo_ref, o_ref):
    x = x_ref[0]
    h = _rms(x, g_ref[...]).astype(BF16)
    merged = None
    for i, y_ref in enumerate((ya_ref, yb_ref, yc_ref)):
        gate = _sigmoid(_dot(h, wg_ref[:, i * D_MODEL:(i + 1) * D_MODEL]))
        t = gate * _dot(y_ref[0], wb_ref[i])
        merged = t if merged is None else merged + t
    o_ref[0] = x + _dot(merged.astype(BF16), wo_ref[...])


def _merge(x, g, ya, yb, yc, wg, wb, wo, tm):
    B, L, _ = x.shape
    row = lambda b, i: (b, i, 0)
    y_spec = pl.BlockSpec((1, tm, 512), row)
    return pl.pallas_call(
        _merge_kernel,
        grid=(B, L // tm),
        in_specs=[pl.BlockSpec((1, tm, D_MODEL), row), _resident((1, D_MODEL)), y_spec, y_spec, y_spec,
                  _resident(wg.shape), _resident(wb.shape), _resident(wo.shape)],
        out_specs=pl.BlockSpec((1, tm, D_MODEL), row),
        out_shape=jax.ShapeDtypeStruct(x.shape, F32),
        compiler_params=_params(2), name="merge_out",
    )(x, g, ya, yb, yc, wg, wb, wo)


def _memkv_kernel(m_ref, g_ref, w_ref, o_ref):
    h = _rms(m_ref[0], g_ref[...]).astype(BF16)
    for c in range(0, 2 * D_MODEL, 2 * MXU_N):
        o_ref[0, :, c:c + 2 * MXU_N] = _dot(h, w_ref[:, c:c + 2 * MXU_N]).astype(o_ref.dtype)


def _memkv(mem, g, w):
    B, M, _ = mem.shape
    return pl.pallas_call(
        _memkv_kernel,
        grid=(B,),
        in_specs=[pl.BlockSpec((1, M, D_MODEL), lambda b: (b, 0, 0)), _resident((1, D_MODEL)),
                  _resident(w.shape)],
        out_specs=pl.BlockSpec((1, M, 2 * D_MODEL), lambda b: (b, 0, 0)),
        out_shape=jax.ShapeDtypeStruct((B, M, 2 * D_MODEL), BF16),
        compiler_params=_params(1), name="mem_kv",
    )(mem, g, w)


def _xattn_kernel(x_ref, g_ref, wq_ref, kv_ref, wo_ref, o_ref):
    x = x_ref[0]
    q = _dot(_rms(x, g_ref[...]).astype(BF16), wq_ref[...]).astype(BF16)
    cols = [slice(h * X_HEAD_DIM, (h + 1) * X_HEAD_DIM) for h in range(X_HEADS)]
    scores = [_dot_nt(q[:, c], kv_ref[0, :, c]) * (X_HEAD_DIM ** -0.5) for c in cols]
    probs = [jnp.exp(s - jnp.max(s, axis=-1, keepdims=True)) for s in scores]
    dens = [jnp.sum(p, axis=-1, keepdims=True) for p in probs]
    outs = [_dot(p.astype(BF16), kv_ref[0, :, D_MODEL + c.start:D_MODEL + c.stop])
            for p, c in zip(probs, cols)]
    attn = jnp.concatenate([(o / d).astype(BF16) for o, d in zip(outs, dens)], axis=1)
    o_ref[0] = x + _dot(attn, wo_ref[...])


def _xattn(x, g, wq, kv, wo, tm):
    B, L, _ = x.shape
    row = lambda b, i: (b, i, 0)
    return pl.pallas_call(
        _xattn_kernel,
        grid=(B, L // tm),
        in_specs=[pl.BlockSpec((1, tm, D_MODEL), row), _resident((1, D_MODEL)), _resident(wq.shape),
                  pl.BlockSpec((1, MEM_LEN, 2 * D_MODEL), lambda b, i: (b, 0, 0)),
                  _resident(wo.shape)],
        out_specs=pl.BlockSpec((1, tm, D_MODEL), row),
        out_shape=jax.ShapeDtypeStruct(x.shape, F32),
        compiler_params=_params(2), name="mem_xattn",
    )(x, g, wq, kv, wo)


def _ffn_kernel(x_ref, g_ref, wgu_ref, wd_ref, gf_ref, o_ref, act_ref, *, final_norm):
    x = x_ref[0]
    h = _rms(x, g_ref[...]).astype(BF16)
    for c in range(0, D_FF, MXU_N):
        gate = _dot(h, wgu_ref[:, c:c + MXU_N])
        up = _dot(h, wgu_ref[:, D_FF + c:D_FF + c + MXU_N])
        act_ref[:, c:c + MXU_N] = (gate * _sigmoid(gate) * up).astype(BF16)
    y = x + _dot(act_ref[...], wd_ref[...])
    if final_norm:
        y = _rms(y, gf_ref[...])
    o_ref[0] = y


def _ffn(x, g, wgu, wd, g_final, final_norm, tm):
    B, L, _ = x.shape
    row = lambda b, i: (b, i, 0)
    return pl.pallas_call(
        functools.partial(_ffn_kernel, final_norm=final_norm),
        grid=(B, L // tm),
        in_specs=[pl.BlockSpec((1, tm, D_MODEL), row), _resident((1, D_MODEL)),
                  _resident(wgu.shape), _resident(wd.shape), _resident((1, D_MODEL))],
        out_specs=pl.BlockSpec((1, tm, D_MODEL), row),
        out_shape=jax.ShapeDtypeStruct(x.shape, F32),
        scratch_shapes=[pltpu.VMEM((tm, D_FF), BF16)],
        compiler_params=_params(2), name="swiglu",
    )(x, g, wgu, wd, g_final)


TM = 1024


def _trunk(x, mem, layers, g_final):
    for l, p in enumerate(layers):
        pa, pr, pn = _proj(x, p["g_mix"], p["w_in"], TM)
        ya = _wattn(pa, p["a_bias"], p["a_sink"])
        yb = _retention(pr, p["r_lg"], p["r_gchunk"])
        yc = _nattn(pn, p["n_bias"])
        x = _merge(x, p["g_mix"], ya, yb, yc, p["w_gate"], p["w_branch"], p["w_mix_out"], TM)
        kv = _memkv(mem, p["g_mem"], p["w_xkv"])
        x = _xattn(x, p["g_xattn"], p["w_xq"], kv, p["w_xo"], TM)
        x = _ffn(x, p["g_ffn"], p["w_gate_up"], p["w_down"], g_final, l == DEPTH - 1, TM)
    return x


def kernel(x_prompt, x_sample, mem_prompt, mem_sample, g_mix, w_in, attn_sink, ret_decay, na_rpb,
           w_branch, w_mix_out, g_xattn, g_mem, w_xq, w_xkv, w_xo, g_ffn, w_gate_up, w_down, g_final):
    layers = []
    for l in range(DEPTH):
        a_bias, a_sink = _wattn_tables(attn_sink[l])
        lg = jax.nn.log_sigmoid(ret_decay[l].astype(F32))
        layers.append(dict(
            g_mix=g_mix[l].reshape(1, D_MODEL),
            w_in=w_in[l, :, :W_MIX].astype(BF16), w_gate=w_in[l, :, W_MIX:].astype(BF16),
            a_bias=a_bias, a_sink=a_sink,
            r_lg=lg, r_gchunk=jnp.exp(R_CHUNK * lg),
            n_bias=_na_bias(na_rpb[l]),
            w_branch=w_branch[l].astype(BF16), w_mix_out=w_mix_out[l].astype(BF16),
            g_xattn=g_xattn[l].reshape(1, D_MODEL), g_mem=g_mem[l].reshape(1, D_MODEL),
            w_xq=w_xq[l].astype(BF16), w_xkv=w_xkv[l].astype(BF16), w_xo=w_xo[l].astype(BF16),
            g_ffn=g_ffn[l].reshape(1, D_MODEL),
            w_gate_up=w_gate_up[l].astype(BF16), w_down=w_down[l].astype(BF16)))
    gf = g_final.reshape(1, D_MODEL)
    return (_trunk(x_prompt, mem_prompt, layers, gf), _trunk(x_sample, mem_sample, layers, gf))
```

```python
import functools

import numpy as np
import jax
import jax.numpy as jnp
from jax import lax
from jax.experimental import pallas as pl
from jax.experimental.pallas import tpu as pltpu

F32 = jnp.float32
BF16 = jnp.bfloat16

D_MODEL = 1024
DEPTH = 2
EPS = 1e-6
HEAD_DIM = 64
A_HEADS = 8
A_KV_HEADS = 2
A_GROUPS = A_HEADS // A_KV_HEADS
A_WINDOW = 128
A_BLOCK = 128
R_HEADS = 4
R_DK = 128
R_DV = 128
R_CHUNK = 128
N_HEADS = 8
GRID_W = 64
NA_ROWS = 8
NA_COLS = 16
MEM_LEN = 256
X_HEADS = 4
X_HEAD_DIM = D_MODEL // X_HEADS
D_FF = 2816
A_Q = A_HEADS * HEAD_DIM
A_KV = A_KV_HEADS * HEAD_DIM
R_W = R_HEADS * R_DK
N_W = N_HEADS * HEAD_DIM
BRANCH_W = 512
assert A_Q == R_W == N_W == BRANCH_W
W_A = A_Q + 2 * A_KV
W_R = 4 * R_W
W_N = 3 * N_W
W_MIX = W_A + W_R + W_N
W_G = 3 * D_MODEL
NEG = -1e30
LOG2E = 1.4426950408889634
SCORE_SCALE = LOG2E * HEAD_DIM ** -0.5

LANE = 128
MXU_N = 256
VMEM_LIMIT = 56 * 1024 * 1024


def _params(n_axes, flags=None):
    return pltpu.CompilerParams(
        dimension_semantics=("arbitrary",) * n_axes, vmem_limit_bytes=VMEM_LIMIT, flags=flags)


def _resident(shape):
    return pl.BlockSpec(shape, lambda *_: (0,) * len(shape), pipeline_mode=pl.Buffered(1))


def _rms(x, g):
    return x * lax.rsqrt(jnp.mean(x * x, axis=-1, keepdims=True) + EPS) * g


def _sigmoid(x):
    return 1.0 / (1.0 + jnp.exp(-x))


def _dot(a, b):
    return jnp.dot(a, b, preferred_element_type=F32)


def _dot_nt(a, b):
    return lax.dot_general(a, b, (((1,), (1,)), ((), ())), preferred_element_type=F32)


def _dot_tn(a, b):
    return lax.dot_general(a, b, (((0,), (0,)), ((), ())), preferred_element_type=F32)


def _proj_kernel(x_ref, g_ref, w_ref, oa_ref, or_ref, on_ref):
    h = _rms(x_ref[0], g_ref[...]).astype(BF16)
    col = 0
    for o_ref, width in ((oa_ref, W_A), (or_ref, W_R), (on_ref, W_N)):
        for c in range(0, width, 2 * MXU_N):
            n = min(2 * MXU_N, width - c)
            o_ref[0, :, c:c + n] = _dot(h, w_ref[:, col + c:col + c + n]).astype(o_ref.dtype)
        col += width


def _proj(x, g, w, tm):
    B, L, _ = x.shape
    row = lambda b, i: (b, i, 0)
    return pl.pallas_call(
        _proj_kernel,
        grid=(B, L // tm),
        in_specs=[pl.BlockSpec((1, tm, D_MODEL), row), _resident((1, D_MODEL)),
                  _resident((D_MODEL, W_MIX))],
        out_specs=[pl.BlockSpec((1, tm, W_A), row), pl.BlockSpec((1, tm, W_R), row),
                   pl.BlockSpec((1, tm, W_N), row)],
        out_shape=[jax.ShapeDtypeStruct((B, L, W_A), BF16), jax.ShapeDtypeStruct((B, L, W_R), F32),
                   jax.ShapeDtypeStruct((B, L, W_N), BF16)],
        compiler_params=_params(2), name="proj_in",
    )(x, g, w)


WA_QB = 8
WA_STAGE = 4
WA_TAIL = 16


def _wattn_kernel(q_ref, kp_ref, kc_ref, kn_ref, vp_ref, vc_ref, vn_ref, bias_ref, sink_ref, ones_ref,
                  o_ref, *, seq_len):
    n = pl.program_id(1)
    win = 3 * A_BLOCK
    k = jnp.concatenate([kp_ref[0], kc_ref[0], kn_ref[0]], axis=0).astype(F32)
    v = jnp.concatenate([vp_ref[0], vc_ref[0], vn_ref[0]], axis=0).astype(F32)
    lo = lax.broadcasted_iota(jnp.int32, k.shape, 1) < HEAD_DIM
    swap = lambda x: jnp.concatenate([x[:, HEAD_DIM:], x[:, :HEAD_DIM]], axis=1)
    ks, vs = swap(k), swap(v)
    k_lo = [jnp.where(lo, k, 0.0).astype(BF16), jnp.where(lo, ks, 0.0).astype(BF16)]
    k_hi = [jnp.where(lo, 0.0, ks).astype(BF16), jnp.where(lo, 0.0, k).astype(BF16)]
    v_lo = [jnp.where(lo, v, 0.0).astype(BF16), jnp.where(lo, vs, 0.0).astype(BF16)]
    v_hi = [jnp.where(lo, 0.0, vs).astype(BF16), jnp.where(lo, 0.0, v).astype(BF16)]
    j = lax.broadcasted_iota(jnp.int32, (win, 1), 0)
    units = [(i, h) for i in range(WA_QB) for h in range(A_KV_HEADS)]
    wrows = lambda i: slice(i * A_BLOCK, i * A_BLOCK + win)

    def scores(i, h):
        rows = slice(i * A_BLOCK, (i + 1) * A_BLOCK)
        c0 = h * A_GROUPS * HEAD_DIM
        qh = jnp.concatenate([q_ref[0, rows, c0:c0 + 2 * HEAD_DIM],
                              q_ref[0, rows, c0 + 2 * HEAD_DIM:c0 + 4 * HEAD_DIM]], axis=0)
        return [_dot_nt(kw[wrows(i)], qh) for kw in (k_lo[h], k_hi[h])]

    def softmax(i, h, par, s):
        s = s + bias_ref[h, par]
        if i in (0, WA_QB - 1):
            kpos = (n * WA_QB + i - 1) * A_BLOCK + j
            s = jnp.where((kpos >= 0) & (kpos < seq_len), s, NEG)
        sk = sink_ref[h, par]
        m = jnp.maximum(jnp.max(s, axis=0, keepdims=True), sk)
        return jnp.exp2(s - m).astype(BF16), jnp.exp2(sk - m)

    for g0 in range(0, len(units), WA_STAGE):
        grp = units[g0:g0 + WA_STAGE]
        all_s = [scores(i, h) for i, h in grp]
        all_p = []
        for (i, h), ss in zip(grp, all_s):
            (pe, ee), (po, eo) = [softmax(i, h, par, s) for par, s in enumerate(ss)]
            tail = jnp.concatenate([ee, eo, jnp.zeros((WA_TAIL - 2, 2 * A_BLOCK), F32)], axis=0)
            all_p.append(jnp.concatenate([pe, po, tail.astype(BF16)], axis=0))
        vtail = jnp.zeros((WA_TAIL, A_KV), BF16)
        all_o = [_dot_tn(p, jnp.concatenate(
                     [jnp.concatenate([v_lo[h][wrows(i)], v_hi[h][wrows(i)], vtail], axis=0),
                      ones_ref[...]], axis=1))
                 for (i, h), p in zip(grp, all_p)]
        for (i, h), res in zip(grp, all_o):
            res = res[:, :A_KV] * (1.0 / res[:, A_KV:])
            c0 = h * A_GROUPS * HEAD_DIM
            o_ref[0, i * A_BLOCK:(i + 1) * A_BLOCK, c0:c0 + 4 * HEAD_DIM] = jnp.concatenate(
                [res[:A_BLOCK], res[A_BLOCK:]], axis=1).astype(o_ref.dtype)


def _wattn(pa, bias, sink):
    B, L, _ = pa.shape
    ones = np.zeros((6 * A_BLOCK + WA_TAIL, A_KV), np.float32)
    ones[:3 * A_BLOCK, :HEAD_DIM] = 1.0
    ones[3 * A_BLOCK:6 * A_BLOCK, HEAD_DIM:] = 1.0
    ones[6 * A_BLOCK, :HEAD_DIM] = 1.0
    ones[6 * A_BLOCK + 1, HEAD_DIM:] = 1.0
    ones = jnp.asarray(ones, BF16)
    nb = L // A_BLOCK
    assert nb % WA_QB == 0
    kcol, vcol = A_Q // A_KV, A_Q // A_KV + 1
    prev = lambda c: pl.BlockSpec((1, A_BLOCK, A_KV), lambda b, n: (b, jnp.maximum(n * WA_QB - 1, 0), c))
    cur = lambda c: pl.BlockSpec((1, WA_QB * A_BLOCK, A_KV), lambda b, n: (b, n, c))
    nxt = lambda c: pl.BlockSpec((1, A_BLOCK, A_KV),
                                 lambda b, n: (b, jnp.minimum((n + 1) * WA_QB, nb - 1), c))
    return pl.pallas_call(
        functools.partial(_wattn_kernel, seq_len=L),
        grid=(B, nb // WA_QB),
        in_specs=[pl.BlockSpec((1, WA_QB * A_BLOCK, A_Q), lambda b, n: (b, n, 0)),
                  prev(kcol), cur(kcol), nxt(kcol), prev(vcol), cur(vcol), nxt(vcol),
                  _resident(bias.shape), _resident(sink.shape), _resident(ones.shape)],
        out_specs=pl.BlockSpec((1, WA_QB * A_BLOCK, A_Q), lambda b, n: (b, n, 0)),
        out_shape=jax.ShapeDtypeStruct((B, L, A_Q), BF16),
        compiler_params=_params(2), name="window_gqa",
    )(pa, pa, pa, pa, pa, pa, pa, bias, sink, ones)


def _wattn_tables(attn_sink):
    i = np.arange(A_BLOCK)[None, :]
    j = np.arange(3 * A_BLOCK)[:, None]
    dist = np.abs(A_BLOCK + i - j)
    slopes = jnp.exp2(-8.0 * jnp.arange(1, A_HEADS + 1, dtype=F32) / A_HEADS)
    bias = -slopes[:, None, None] * jnp.asarray(dist, F32)[None]
    bias = jnp.where(jnp.asarray(dist <= A_WINDOW)[None], bias * LOG2E, NEG)
    bias = bias.reshape(A_KV_HEADS, 2, 2, 3 * A_BLOCK, A_BLOCK).transpose(0, 2, 3, 1, 4)
    bias = bias.reshape(A_KV_HEADS, 2, 3 * A_BLOCK, 2 * A_BLOCK)
    sink = jnp.repeat(attn_sink.astype(F32) * LOG2E, A_BLOCK).reshape(A_KV_HEADS, 2, 2, A_BLOCK)
    sink = sink.transpose(0, 2, 1, 3).reshape(A_KV_HEADS, 2, 1, 2 * A_BLOCK)
    return bias, sink


RET_UNROLL = 16


def _ret_kernel(lg_ref, gc_ref, q_ref, k_ref, v_ref, g_ref, o_ref, st_ref, *, nc):
    h = pl.program_id(1)
    C = R_CHUNK
    lgf, lgb = lg_ref[0, h], lg_ref[1, h]
    gcf, gcb = gc_ref[0, h], gc_ref[1, h]
    diff = (lax.broadcasted_iota(jnp.int32, (C, C), 0)
            - lax.broadcasted_iota(jnp.int32, (C, C), 1)).astype(F32)
    decay = (jnp.where(diff >= 0, jnp.exp(jnp.maximum(diff, 0.0) * lgf), 0.0)
             + jnp.where(diff <= 0, jnp.exp(jnp.maximum(-diff, 0.0) * lgb), 0.0))
    idx = lax.broadcasted_iota(jnp.int32, (C, R_DK), 0).astype(F32)
    wq2 = jnp.concatenate([jnp.exp((idx + 1.0) * lgf), jnp.exp((C - idx) * lgb)], axis=1)
    wk2 = jnp.concatenate([jnp.exp((C - 1.0 - idx) * lgf), jnp.exp(idx * lgb)], axis=1)

    def rows(n):
        return pl.ds(pl.multiple_of(n * C, C), C)

    def kv_body(i, carry):
        for u in range(RET_UNROLL):
            n = i * RET_UNROLL + u
            k = k_ref[0, rows(n), :] * (R_DK ** -0.5)
            kw = (jnp.concatenate([k, k], axis=1) * wk2).astype(BF16)
            st_ref[n] = _dot_tn(kw, v_ref[0, rows(n), :].astype(BF16))
        return carry

    lax.fori_loop(0, nc // RET_UNROLL, kv_body, 0)

    def scan_body(t, carry):
        sf, sb = carry
        nb = nc - 1 - t
        kvf = st_ref[t, :R_DK, :]
        st_ref[t, :R_DK, :] = sf
        kvb = st_ref[nb, R_DK:, :]
        st_ref[nb, R_DK:, :] = sb
        return sf * gcf + kvf, sb * gcb + kvb

    zero = jnp.zeros((R_DK, R_DV), F32)
    lax.fori_loop(0, nc, scan_body, (zero, zero))

    def out_body(i, carry):
        ns = [i * RET_UNROLL + u for u in range(RET_UNROLL)]
        qs = [q_ref[0, rows(n), :] for n in ns]
        inner = [_dot_nt(q.astype(BF16), (k_ref[0, rows(n), :] * (R_DK ** -0.5)).astype(BF16))
                 for n, q in zip(ns, qs)]
        lhs = [jnp.concatenate([s * decay, jnp.concatenate([q, q], axis=1) * wq2], axis=1).astype(BF16)
               for s, q in zip(inner, qs)]
        outs = [_dot(a, jnp.concatenate([v_ref[0, rows(n), :].astype(BF16),
                                         st_ref[n].astype(BF16)], axis=0))
                for n, a in zip(ns, lhs)]
        for n, o in zip(ns, outs):
            o = o * lax.rsqrt(jnp.mean(o * o, axis=-1, keepdims=True) + EPS)
            g = g_ref[0, rows(n), :]
            o_ref[0, rows(n), :] = (g * _sigmoid(g) * o).astype(o_ref.dtype)
        return carry

    lax.fori_loop(0, nc // RET_UNROLL, out_body, 0)


def _retention(pr, lg, gchunk):
    B, L, _ = pr.shape
    nc = L // R_CHUNK
    assert nc % RET_UNROLL == 0
    part = lambda p: pl.BlockSpec((1, L, R_DK), lambda b, h: (b, 0, p * R_HEADS + h))
    smem = pl.BlockSpec(memory_space=pltpu.SMEM)
    return pl.pallas_call(
        functools.partial(_ret_kernel, nc=nc),
        grid=(B, R_HEADS),
        in_specs=[smem, smem, part(0), part(1), part(2), part(3)],
        out_specs=pl.BlockSpec((1, L, R_DV), lambda b, h: (b, 0, h)),
        out_shape=jax.ShapeDtypeStruct((B, L, R_W), BF16),
        scratch_shapes=[pltpu.VMEM((nc, 2 * R_DK, R_DV), F32)],
        compiler_params=_params(2), name="retention",
    )(lg, gchunk, pr, pr, pr, pr)


NA_HG = 4
NA_UNROLL = 16


def _na_kernel(q_ref, k_ref, v_ref, bias_ref, o_ref, *, n_rows):
    keys = NA_ROWS * GRID_W
    head = lax.broadcasted_iota(jnp.int32, (GRID_W, NA_HG * HEAD_DIM), 1) // HEAD_DIM

    def qrows(r):
        return pl.ds(pl.multiple_of(r * GRID_W, GRID_W), GRID_W)

    def krows(r):
        rs = jnp.clip(r - NA_ROWS // 2, 0, n_rows - NA_ROWS)
        return pl.ds(pl.multiple_of(rs * GRID_W, GRID_W), keys)

    def scores(r):
        q = q_ref[0, qrows(r), :].astype(F32)
        qbd = jnp.concatenate([jnp.where(head == h, q, 0.0) for h in range(NA_HG)], axis=0).astype(BF16)
        return _dot_nt(k_ref[0, krows(r), :], qbd)

    def softmax(r, s):
        rs = jnp.clip(r - NA_ROWS // 2, 0, n_rows - NA_ROWS)
        s = s + jnp.concatenate(
            [bias_ref[0, rs - r + (NA_ROWS - 1) + j] for j in range(NA_ROWS)], axis=0)
        p = jnp.exp2(s - jnp.max(s, axis=0, keepdims=True))
        return (p * (1.0 / jnp.sum(p, axis=0, keepdims=True))).astype(BF16)

    def body(i, carry):
        rs_ = [i * NA_UNROLL + u for u in range(NA_UNROLL)]
        all_s = [scores(r) for r in rs_]
        all_p = [softmax(r, s) for r, s in zip(rs_, all_s)]
        all_o = [_dot_tn(p, v_ref[0, krows(r), :]) for r, p in zip(rs_, all_p)]
        for r, o in zip(rs_, all_o):
            out = o[:GRID_W]
            for h in range(1, NA_HG):
                out = jnp.where(head == h, o[h * GRID_W:(h + 1) * GRID_W], out)
            o_ref[0, qrows(r), :] = out.astype(o_ref.dtype)
        return carry

    lax.fori_loop(0, n_rows // NA_UNROLL, body, 0)


def _nattn(pn, bias):
    B, L, _ = pn.shape
    groups = N_HEADS // NA_HG
    assert (L // GRID_W) % NA_UNROLL == 0
    width = NA_HG * HEAD_DIM
    part = lambda p: pl.BlockSpec((1, L, width), lambda b, g: (b, 0, p * groups + g))
    return pl.pallas_call(
        functools.partial(_na_kernel, n_rows=L // GRID_W),
        grid=(B, groups),
        in_specs=[part(0), part(1), part(2),
                  pl.BlockSpec((1,) + bias.shape[1:], lambda b, g: (g, 0, 0, 0))],
        out_specs=pl.BlockSpec((1, L, width), lambda b, g: (b, 0, g)),
        out_shape=jax.ShapeDtypeStruct((B, L, N_W), BF16),
        compiler_params=_params(2), name="neighbourhood_attn",
    )(pn, pn, pn, bias)


def _na_bias(rpb):
    c = np.arange(GRID_W)
    cs = np.clip(c - NA_COLS // 2, 0, GRID_W - NA_COLS)
    valid = (c[:, None] >= cs[None, :]) & (c[:, None] < cs[None, :] + NA_COLS)
    a = rpb.astype(F32)
    period = 2 * GRID_W
    z = jnp.concatenate([a[..., NA_COLS - 1::-1],
                         jnp.full(a.shape[:-1] + (period - (2 * NA_COLS - 1),), NEG, F32),
                         a[..., :NA_COLS - 1:-1]], axis=-1)
    t = jnp.tile(z, GRID_W)[..., :GRID_W * (period - 1)]
    t = t.reshape(a.shape[:-1] + (GRID_W, period - 1))[..., :GRID_W]
    t = jnp.where(jnp.asarray(valid)[None, None], t * LOG2E, NEG)
    t = t.reshape(N_HEADS // NA_HG, NA_HG, 2 * NA_ROWS - 1, GRID_W, GRID_W)
    return t.transpose(0, 2, 3, 1, 4).reshape(N_HEADS // NA_HG, 2 * NA_ROWS - 1, GRID_W, NA_HG * GRID_W)


def _merge_kernel(x_ref, g_ref, ya_ref, yb_ref, yc_ref, wg_ref, wb_ref, wo_ref, o_ref):
    x = x_ref[0]
    h = _rms(x, g_ref[...]).astype(BF16)
    merged = None
    for i, y_ref in enumerate((ya_ref, yb_ref, yc_ref)):
        gate = _sigmoid(_dot(h, wg_ref[:, i * D_MODEL:(i + 1) * D_MODEL]))
        t = gate * _dot(y_ref[0], wb_ref[i])
        merged = t if merged is None else merged + t
    o_ref[0] = x + _dot(merged.astype(BF16), wo_ref[...])


def _merge(x, g, ya, yb, yc, wg, wb, wo, tm):
    B, L, _ = x.shape
    row = lambda b, i: (b, i, 0)
    y_spec = pl.BlockSpec((1, tm, BRANCH_W), row)
    return pl.pallas_call(
        _merge_kernel,
        grid=(B, L // tm),
        in_specs=[pl.BlockSpec((1, tm, D_MODEL), row), _resident((1, D_MODEL)), y_spec, y_spec, y_spec,
                  _resident(wg.shape), _resident(wb.shape), _resident(wo.shape)],
        out_specs=pl.BlockSpec((1, tm, D_MODEL), row),
        out_shape=jax.ShapeDtypeStruct(x.shape, F32),
        compiler_params=_params(2), name="merge_out",
    )(x, g, ya, yb, yc, wg, wb, wo)


def _memkv_kernel(m_ref, g_ref, w_ref, o_ref):
    h = _rms(m_ref[0], g_ref[...]).astype(BF16)
    for c in range(0, 2 * D_MODEL, 2 * MXU_N):
        o_ref[0, :, c:c + 2 * MXU_N] = _dot(h, w_ref[:, c:c + 2 * MXU_N]).astype(o_ref.dtype)


def _memkv(mem, g, w):
    B, M, _ = mem.shape
    return pl.pallas_call(
        _memkv_kernel,
        grid=(B,),
        in_specs=[pl.BlockSpec((1, M, D_MODEL), lambda b: (b, 0, 0)), _resident((1, D_MODEL)),
                  _resident(w.shape)],
        out_specs=pl.BlockSpec((1, M, 2 * D_MODEL), lambda b: (b, 0, 0)),
        out_shape=jax.ShapeDtypeStruct((B, M, 2 * D_MODEL), BF16),
        compiler_params=_params(1), name="mem_kv",
    )(mem, g, w)


def _xattn_kernel(x_ref, g_ref, wq_ref, kv_ref, wo_ref, o_ref):
    x = x_ref[0]
    q = _dot(_rms(x, g_ref[...]).astype(BF16), wq_ref[...]).astype(BF16)
    cols = [slice(h * X_HEAD_DIM, (h + 1) * X_HEAD_DIM) for h in range(X_HEADS)]
    scores = [_dot_nt(q[:, c], kv_ref[0, :, c]) * (X_HEAD_DIM ** -0.5) for c in cols]
    probs = [jnp.exp(s - jnp.max(s, axis=-1, keepdims=True)) for s in scores]
    dens = [jnp.sum(p, axis=-1, keepdims=True) for p in probs]
    outs = [_dot(p.astype(BF16), kv_ref[0, :, D_MODEL + c.start:D_MODEL + c.stop])
            for p, c in zip(probs, cols)]
    attn = jnp.concatenate([(o / d).astype(BF16) for o, d in zip(outs, dens)], axis=1)
    o_ref[0] = x + _dot(attn, wo_ref[...])


def _xattn(x, g, wq, kv, wo, tm):
    B, L, _ = x.shape
    row = lambda b, i: (b, i, 0)
    return pl.pallas_call(
        _xattn_kernel,
        grid=(B, L // tm),
        in_specs=[pl.BlockSpec((1, tm, D_MODEL), row), _resident((1, D_MODEL)), _resident(wq.shape),
                  pl.BlockSpec((1, MEM_LEN, 2 * D_MODEL), lambda b, i: (b, 0, 0)),
                  _resident(wo.shape)],
        out_specs=pl.BlockSpec((1, tm, D_MODEL), row),
        out_shape=jax.ShapeDtypeStruct(x.shape, F32),
        compiler_params=_params(2), name="mem_xattn",
    )(x, g, wq, kv, wo)


def _ffn_kernel(x_ref, g_ref, wgu_ref, wd_ref, gf_ref, o_ref, act_ref, *, final_norm):
    x = x_ref[0]
    h = _rms(x, g_ref[...]).astype(BF16)
    for c in range(0, D_FF, MXU_N):
        gate = _dot(h, wgu_ref[:, c:c + MXU_N])
        up = _dot(h, wgu_ref[:, D_FF + c:D_FF + c + MXU_N])
        act_ref[:, c:c + MXU_N] = (gate * _sigmoid(gate) * up).astype(BF16)
    y = x + _dot(act_ref[...], wd_ref[...])
    if final_norm:
        y = _rms(y, gf_ref[...])
    o_ref[0] = y


def _ffn(x, g, wgu, wd, g_final, final_norm, tm):
    B, L, _ = x.shape
    row = lambda b, i: (b, i, 0)
    return pl.pallas_call(
        functools.partial(_ffn_kernel, final_norm=final_norm),
        grid=(B, L // tm),
        in_specs=[pl.BlockSpec((1, tm, D_MODEL), row), _resident((1, D_MODEL)),
                  _resident(wgu.shape), _resident(wd.shape), _resident((1, D_MODEL))],
        out_specs=pl.BlockSpec((1, tm, D_MODEL), row),
        out_shape=jax.ShapeDtypeStruct(x.shape, F32),
        scratch_shapes=[pltpu.VMEM((tm, D_FF), BF16)],
        compiler_params=_params(2), name="swiglu",
    )(x, g, wgu, wd, g_final)


TM = 1024


def _trunk(x, mem, layers, g_final):
    for l, p in enumerate(layers):
        pa, pr, pn = _proj(x, p["g_mix"], p["w_in"], TM)
        ya = _wattn(pa, p["a_bias"], p["a_sink"])
        yb = _retention(pr, p["r_lg"], p["r_gchunk"])
        yc = _nattn(pn, p["n_bias"])
        x = _merge(x, p["g_mix"], ya, yb, yc, p["w_gate"], p["w_branch"], p["w_mix_out"], TM)
        kv = _memkv(mem, p["g_mem"], p["w_xkv"])
        x = _xattn(x, p["g_xattn"], p["w_xq"], kv, p["w_xo"], TM)
        x = _ffn(x, p["g_ffn"], p["w_gate_up"], p["w_down"], g_final, l == DEPTH - 1, TM)
    return x


def kernel(x_prompt, x_sample, mem_prompt, mem_sample, g_mix, w_in, attn_sink, ret_decay, na_rpb,
           w_branch, w_mix_out, g_xattn, g_mem, w_xq, w_xkv, w_xo, g_ffn, w_gate_up, w_down, g_final):
    col_scale = np.ones((W_MIX,), np.float32)
    col_scale[A_Q:A_Q + A_KV] = SCORE_SCALE
    col_scale[W_A + W_R:W_A + W_R + N_W] = SCORE_SCALE
    col_scale = jnp.asarray(col_scale)
    layers = []
    for l in range(DEPTH):
        a_bias, a_sink = _wattn_tables(attn_sink[l])
        lg = jax.nn.log_sigmoid(ret_decay[l].astype(F32))
        layers.append(dict(
            g_mix=g_mix[l].reshape(1, D_MODEL),
            w_in=(w_in[l, :, :W_MIX] * col_scale).astype(BF16), w_gate=w_in[l, :, W_MIX:].astype(BF16),
            a_bias=a_bias, a_sink=a_sink,
            r_lg=lg, r_gchunk=jnp.exp(R_CHUNK * lg),
            n_bias=_na_bias(na_rpb[l]),
            w_branch=w_branch[l].astype(BF16), w_mix_out=w_mix_out[l].astype(BF16),
            g_xattn=g_xattn[l].reshape(1, D_MODEL), g_mem=g_mem[l].reshape(1, D_MODEL),
            w_xq=w_xq[l].astype(BF16), w_xkv=w_xkv[l].astype(BF16), w_xo=w_xo[l].astype(BF16),
            g_ffn=g_ffn[l].reshape(1, D_MODEL),
            w_gate_up=w_gate_up[l].astype(BF16), w_down=w_down[l].astype(BF16)))
    gf = g_final.reshape(1, D_MODEL)
    return (_trunk(x_prompt, mem_prompt, layers, gf), _trunk(x_sample, mem_sample, layers, gf))
```

```python
import functools

import numpy as np
import jax
import jax.numpy as jnp
from jax import lax
from jax.experimental import pallas as pl
from jax.experimental.pallas import tpu as pltpu

F32 = jnp.float32
BF16 = jnp.bfloat16

D_MODEL = 1024
DEPTH = 2
EPS = 1e-6
HEAD_DIM = 64
A_HEADS = 8
A_KV_HEADS = 2
A_GROUPS = A_HEADS // A_KV_HEADS
A_WINDOW = 128
A_BLOCK = 128
R_HEADS = 4
R_DK = 128
R_DV = 128
R_CHUNK = 128
N_HEADS = 8
GRID_W = 64
NA_ROWS = 8
NA_COLS = 16
MEM_LEN = 256
X_HEADS = 4
X_HEAD_DIM = D_MODEL // X_HEADS
D_FF = 2816
A_Q = A_HEADS * HEAD_DIM
A_KV = A_KV_HEADS * HEAD_DIM
R_W = R_HEADS * R_DK
N_W = N_HEADS * HEAD_DIM
BRANCH_W = 512
assert A_Q == R_W == N_W == BRANCH_W
W_A = A_Q + 2 * A_KV
W_R = 4 * R_W
W_N = 3 * N_W
W_MIX = W_A + W_R + W_N
NEG = -1e30
LOG2E = 1.4426950408889634
SCORE_SCALE = LOG2E * HEAD_DIM ** -0.5

MXU_N = 256
VMEM_LIMIT = 56 * 1024 * 1024


def _params(n_axes):
    return pltpu.CompilerParams(
        dimension_semantics=("arbitrary",) * n_axes, vmem_limit_bytes=VMEM_LIMIT)


def _resident(shape):
    return pl.BlockSpec(shape, lambda *_: (0,) * len(shape), pipeline_mode=pl.Buffered(1))


def _rms(x, g):
    return x * lax.rsqrt(jnp.mean(x * x, axis=-1, keepdims=True) + EPS) * g


def _sigmoid(x):
    return 1.0 / (1.0 + jnp.exp(-x))


def _dot(a, b):
    return jnp.dot(a, b, preferred_element_type=F32)


def _dot_nt(a, b):
    return lax.dot_general(a, b, (((1,), (1,)), ((), ())), preferred_element_type=F32)


def _dot_tn(a, b):
    return lax.dot_general(a, b, (((0,), (0,)), ((), ())), preferred_element_type=F32)


def _proj_kernel(x_ref, g_ref, w_ref, oa_ref, or_ref, on_ref):
    h = _rms(x_ref[0], g_ref[...]).astype(BF16)
    col = 0
    for o_ref, width in ((oa_ref, W_A), (or_ref, W_R), (on_ref, W_N)):
        for c in range(0, width, 2 * MXU_N):
            n = min(2 * MXU_N, width - c)
            o_ref[0, :, c:c + n] = _dot(h, w_ref[:, col + c:col + c + n]).astype(o_ref.dtype)
        col += width


def _proj(x, g, w, tm):
    B, L, _ = x.shape
    row = lambda b, i: (b, i, 0)
    return pl.pallas_call(
        _proj_kernel,
        grid=(B, L // tm),
        in_specs=[pl.BlockSpec((1, tm, D_MODEL), row), _resident((1, D_MODEL)),
                  _resident((D_MODEL, W_MIX))],
        out_specs=[pl.BlockSpec((1, tm, W_A), row), pl.BlockSpec((1, tm, W_R), row),
                   pl.BlockSpec((1, tm, W_N), row)],
        out_shape=[jax.ShapeDtypeStruct((B, L, W_A), BF16), jax.ShapeDtypeStruct((B, L, W_R), F32),
                   jax.ShapeDtypeStruct((B, L, W_N), BF16)],
        compiler_params=_params(2), name="proj_in",
    )(x, g, w)


WA_QB = 8
WA_STAGE = 4


def _wattn_kernel(q_ref, kp_ref, kc_ref, kn_ref, vp_ref, vc_ref, vn_ref, bias_ref, sink_ref, ones_ref,
                  o_ref, *, seq_len):
    n = pl.program_id(1)
    win = 3 * A_BLOCK
    k = jnp.concatenate([kp_ref[0], kc_ref[0], kn_ref[0]], axis=0).astype(F32)
    v = jnp.concatenate([vp_ref[0], vc_ref[0], vn_ref[0]], axis=0).astype(F32)
    lo = lax.broadcasted_iota(jnp.int32, k.shape, 1) < HEAD_DIM
    swap = lambda x: jnp.concatenate([x[:, HEAD_DIM:], x[:, :HEAD_DIM]], axis=1)
    ks, vs = swap(k), swap(v)
    k_lo = [jnp.where(lo, k, 0.0).astype(BF16), jnp.where(lo, ks, 0.0).astype(BF16)]
    k_hi = [jnp.where(lo, 0.0, ks).astype(BF16), jnp.where(lo, 0.0, k).astype(BF16)]
    v_lo = [jnp.where(lo, v, 0.0).astype(BF16), jnp.where(lo, vs, 0.0).astype(BF16)]
    v_hi = [jnp.where(lo, 0.0, vs).astype(BF16), jnp.where(lo, 0.0, v).astype(BF16)]
    j = lax.broadcasted_iota(jnp.int32, (win, 1), 0)
    even_rows = lax.broadcasted_iota(jnp.int32, (A_KV, 2 * A_BLOCK), 0) < HEAD_DIM
    units = [(i, h) for i in range(WA_QB) for h in range(A_KV_HEADS)]
    wrows = lambda i: slice(i * A_BLOCK, i * A_BLOCK + win)

    def scores(i, h):
        rows = slice(i * A_BLOCK, (i + 1) * A_BLOCK)
        c0 = h * A_GROUPS * HEAD_DIM
        qh = jnp.concatenate([q_ref[0, rows, c0:c0 + 2 * HEAD_DIM],
                              q_ref[0, rows, c0 + 2 * HEAD_DIM:c0 + 4 * HEAD_DIM]], axis=0)
        return [_dot_nt(kw[wrows(i)], qh) for kw in (k_lo[h], k_hi[h])]

    def softmax(i, h, par, s):
        s = s + bias_ref[h, par]
        if i in (0, WA_QB - 1):
            kpos = (n * WA_QB + i - 1) * A_BLOCK + j
            s = jnp.where((kpos >= 0) & (kpos < seq_len), s, NEG)
        sk = sink_ref[h, par]
        m = jnp.maximum(jnp.max(s, axis=0, keepdims=True), sk)
        return jnp.exp2(s - m).astype(BF16), jnp.exp2(sk - m)

    for g0 in range(0, len(units), WA_STAGE):
        grp = units[g0:g0 + WA_STAGE]
        all_s = [scores(i, h) for i, h in grp]
        all_p, all_e = [], []
        for (i, h), ss in zip(grp, all_s):
            (pe, ee), (po, eo) = [softmax(i, h, par, s) for par, s in enumerate(ss)]
            all_p.append(jnp.concatenate([pe, po], axis=0))
            all_e.append(jnp.where(even_rows, ee, eo))
        all_o = [_dot_tn(p, jnp.concatenate(
                     [jnp.concatenate([v_lo[h][wrows(i)], v_hi[h][wrows(i)]], axis=0), ones_ref[...]], axis=1))
                 for (i, h), p in zip(grp, all_p)]
        for (i, h), res, e in zip(grp, all_o, all_e):
            res = res[:, :A_KV] * (1.0 / (res[:, A_KV:] + jnp.transpose(e)))
            c0 = h * A_GROUPS * HEAD_DIM
            o_ref[0, i * A_BLOCK:(i + 1) * A_BLOCK, c0:c0 + 4 * HEAD_DIM] = jnp.concatenate(
                [res[:A_BLOCK], res[A_BLOCK:]], axis=1).astype(o_ref.dtype)


def _wattn(pa, bias, sink):
    B, L, _ = pa.shape
    ones = np.zeros((6 * A_BLOCK, A_KV), np.float32)
    ones[:3 * A_BLOCK, :HEAD_DIM] = 1.0
    ones[3 * A_BLOCK:, HEAD_DIM:] = 1.0
    ones = jnp.asarray(ones, BF16)
    nb = L // A_BLOCK
    assert nb % WA_QB == 0
    kcol, vcol = A_Q // A_KV, A_Q // A_KV + 1
    prev = lambda c: pl.BlockSpec((1, A_BLOCK, A_KV), lambda b, n: (b, jnp.maximum(n * WA_QB - 1, 0), c))
    cur = lambda c: pl.BlockSpec((1, WA_QB * A_BLOCK, A_KV), lambda b, n: (b, n, c))
    nxt = lambda c: pl.BlockSpec((1, A_BLOCK, A_KV),
                                 lambda b, n: (b, jnp.minimum((n + 1) * WA_QB, nb - 1), c))
    return pl.pallas_call(
        functools.partial(_wattn_kernel, seq_len=L),
        grid=(B, nb // WA_QB),
        in_specs=[pl.BlockSpec((1, WA_QB * A_BLOCK, A_Q), lambda b, n: (b, n, 0)),
                  prev(kcol), cur(kcol), nxt(kcol), prev(vcol), cur(vcol), nxt(vcol),
                  _resident(bias.shape), _resident(sink.shape), _resident(ones.shape)],
        out_specs=pl.BlockSpec((1, WA_QB * A_BLOCK, A_Q), lambda b, n: (b, n, 0)),
        out_shape=jax.ShapeDtypeStruct((B, L, A_Q), BF16),
        compiler_params=_params(2), name="window_gqa",
    )(pa, pa, pa, pa, pa, pa, pa, bias, sink, ones)


def _wattn_tables(attn_sink):
    i = np.arange(A_BLOCK)[None, :]
    j = np.arange(3 * A_BLOCK)[:, None]
    dist = np.abs(A_BLOCK + i - j)
    slopes = jnp.exp2(-8.0 * jnp.arange(1, A_HEADS + 1, dtype=F32) / A_HEADS)
    bias = -slopes[:, None, None] * jnp.asarray(dist, F32)[None]
    bias = jnp.where(jnp.asarray(dist <= A_WINDOW)[None], bias * LOG2E, NEG)
    bias = bias.reshape(A_KV_HEADS, 2, 2, 3 * A_BLOCK, A_BLOCK).transpose(0, 2, 3, 1, 4)
    bias = bias.reshape(A_KV_HEADS, 2, 3 * A_BLOCK, 2 * A_BLOCK)
    sink = jnp.repeat(attn_sink.astype(F32) * LOG2E, A_BLOCK).reshape(A_KV_HEADS, 2, 2, A_BLOCK)
    sink = sink.transpose(0, 2, 1, 3).reshape(A_KV_HEADS, 2, 1, 2 * A_BLOCK)
    return bias, sink


RET_UNROLL = 16


def _ret_kernel(lg_ref, gc_ref, q_ref, k_ref, v_ref, g_ref, o_ref, st_ref, *, nc):
    h = pl.program_id(1)
    C = R_CHUNK
    lgf, lgb = lg_ref[0, h], lg_ref[1, h]
    gcf, gcb = gc_ref[0, h], gc_ref[1, h]
    diff = (lax.broadcasted_iota(jnp.int32, (C, C), 0)
            - lax.broadcasted_iota(jnp.int32, (C, C), 1)).astype(F32)
    decay = (jnp.where(diff >= 0, jnp.exp(jnp.maximum(diff, 0.0) * lgf), 0.0)
             + jnp.where(diff <= 0, jnp.exp(jnp.maximum(-diff, 0.0) * lgb), 0.0))
    idx = lax.broadcasted_iota(jnp.int32, (C, R_DK), 0).astype(F32)
    wq2 = jnp.concatenate([jnp.exp((idx + 1.0) * lgf), jnp.exp((C - idx) * lgb)], axis=1)
    wk2 = jnp.concatenate([jnp.exp((C - 1.0 - idx) * lgf), jnp.exp(idx * lgb)], axis=1)

    def rows(n):
        return pl.ds(pl.multiple_of(n * C, C), C)

    def kv_body(i, carry):
        for u in range(RET_UNROLL):
            n = i * RET_UNROLL + u
            k = k_ref[0, rows(n), :] * (R_DK ** -0.5)
            kw = (jnp.concatenate([k, k], axis=1) * wk2).astype(BF16)
            st_ref[n] = _dot_tn(kw, v_ref[0, rows(n), :].astype(BF16))
        return carry

    lax.fori_loop(0, nc // RET_UNROLL, kv_body, 0)

    def scan_body(t, carry):
        sf, sb = carry
        nb = nc - 1 - t
        kvf = st_ref[t, :R_DK, :]
        st_ref[t, :R_DK, :] = sf
        kvb = st_ref[nb, R_DK:, :]
        st_ref[nb, R_DK:, :] = sb
        return sf * gcf + kvf, sb * gcb + kvb

    zero = jnp.zeros((R_DK, R_DV), F32)
    lax.fori_loop(0, nc, scan_body, (zero, zero))

    def out_body(i, carry):
        ns = [i * RET_UNROLL + u for u in range(RET_UNROLL)]
        qs = [q_ref[0, rows(n), :] for n in ns]
        inner = [_dot_nt(q.astype(BF16), (k_ref[0, rows(n), :] * (R_DK ** -0.5)).astype(BF16))
                 for n, q in zip(ns, qs)]
        lhs = [jnp.concatenate([s * decay, jnp.concatenate([q, q], axis=1) * wq2], axis=1).astype(BF16)
               for s, q in zip(inner, qs)]
        outs = [_dot(a, jnp.concatenate([v_ref[0, rows(n), :].astype(BF16),
                                         st_ref[n].astype(BF16)], axis=0))
                for n, a in zip(ns, lhs)]
        for n, o in zip(ns, outs):
            o = o * lax.rsqrt(jnp.mean(o * o, axis=-1, keepdims=True) + EPS)
            g = g_ref[0, rows(n), :]
            o_ref[0, rows(n), :] = (g * _sigmoid(g) * o).astype(o_ref.dtype)
        return carry

    lax.fori_loop(0, nc // RET_UNROLL, out_body, 0)


def _retention(pr, lg, gchunk):
    B, L, _ = pr.shape
    nc = L // R_CHUNK
    assert nc % RET_UNROLL == 0
    part = lambda p: pl.BlockSpec((1, L, R_DK), lambda b, h: (b, 0, p * R_HEADS + h))
    smem = pl.BlockSpec(memory_space=pltpu.SMEM)
    return pl.pallas_call(
        functools.partial(_ret_kernel, nc=nc),
        grid=(B, R_HEADS),
        in_specs=[smem, smem, part(0), part(1), part(2), part(3)],
        out_specs=pl.BlockSpec((1, L, R_DV), lambda b, h: (b, 0, h)),
        out_shape=jax.ShapeDtypeStruct((B, L, R_W), BF16),
        scratch_shapes=[pltpu.VMEM((nc, 2 * R_DK, R_DV), F32)],
        compiler_params=_params(2), name="retention",
    )(lg, gchunk, pr, pr, pr, pr)


NA_HG = 4
NA_UNROLL = 16


def _na_kernel(q_ref, k_ref, v_ref, bias_ref, o_ref, *, n_rows):
    keys = NA_ROWS * GRID_W
    head = lax.broadcasted_iota(jnp.int32, (GRID_W, NA_HG * HEAD_DIM), 1) // HEAD_DIM

    def qrows(r):
        return pl.ds(pl.multiple_of(r * GRID_W, GRID_W), GRID_W)

    def krows(r):
        rs = jnp.clip(r - NA_ROWS // 2, 0, n_rows - NA_ROWS)
        return pl.ds(pl.multiple_of(rs * GRID_W, GRID_W), keys)

    def scores(r):
        q = q_ref[0, qrows(r), :].astype(F32)
        qbd = jnp.concatenate([jnp.where(head == h, q, 0.0) for h in range(NA_HG)], axis=0).astype(BF16)
        return _dot_nt(k_ref[0, krows(r), :], qbd)

    def softmax(r, s):
        rs = jnp.clip(r - NA_ROWS // 2, 0, n_rows - NA_ROWS)
        s = s + jnp.concatenate(
            [bias_ref[0, rs - r + (NA_ROWS - 1) + j] for j in range(NA_ROWS)], axis=0)
        p = jnp.exp2(s - jnp.max(s, axis=0, keepdims=True))
        return (p * (1.0 / jnp.sum(p, axis=0, keepdims=True))).astype(BF16)

    def body(i, carry):
        rs_ = [i * NA_UNROLL + u for u in range(NA_UNROLL)]
        all_s = [scores(r) for r in rs_]
        all_p = [softmax(r, s) for r, s in zip(rs_, all_s)]
        all_o = [_dot_tn(p, v_ref[0, krows(r), :]) for r, p in zip(rs_, all_p)]
        for r, o in zip(rs_, all_o):
            out = o[:GRID_W]
            for h in range(1, NA_HG):
                out = jnp.where(head == h, o[h * GRID_W:(h + 1) * GRID_W], out)
            o_ref[0, qrows(r), :] = out.astype(o_ref.dtype)
        return carry

    lax.fori_loop(0, n_rows // NA_UNROLL, body, 0)


def _nattn(pn, bias):
    B, L, _ = pn.shape
    groups = N_HEADS // NA_HG
    assert (L // GRID_W) % NA_UNROLL == 0
    width = NA_HG * HEAD_DIM
    part = lambda p: pl.BlockSpec((1, L, width), lambda b, g: (b, 0, p * groups + g))
    return pl.pallas_call(
        functools.partial(_na_kernel, n_rows=L // GRID_W),
        grid=(B, groups),
        in_specs=[part(0), part(1), part(2),
                  pl.BlockSpec((1,) + bias.shape[1:], lambda b, g: (g, 0, 0, 0))],
        out_specs=pl.BlockSpec((1, L, width), lambda b, g: (b, 0, g)),
        out_shape=jax.ShapeDtypeStruct((B, L, N_W), BF16),
        compiler_params=_params(2), name="neighbourhood_attn",
    )(pn, pn, pn, bias)


def _na_bias(rpb):
    c = np.arange(GRID_W)
    cs = np.clip(c - NA_COLS // 2, 0, GRID_W - NA_COLS)
    valid = (c[:, None] >= cs[None, :]) & (c[:, None] < cs[None, :] + NA_COLS)
    a = rpb.astype(F32)
    period = 2 * GRID_W
    z = jnp.concatenate([a[..., NA_COLS - 1::-1],
                         jnp.full(a.shape[:-1] + (period - (2 * NA_COLS - 1),), NEG, F32),
                         a[..., :NA_COLS - 1:-1]], axis=-1)
    t = jnp.tile(z, GRID_W)[..., :GRID_W * (period - 1)]
    t = t.reshape(a.shape[:-1] + (GRID_W, period - 1))[..., :GRID_W]
    t = jnp.where(jnp.asarray(valid)[None, None], t * LOG2E, NEG)
    t = t.reshape(N_HEADS // NA_HG, NA_HG, 2 * NA_ROWS - 1, GRID_W, GRID_W)
    return t.transpose(0, 2, 3, 1, 4).reshape(N_HEADS // NA_HG, 2 * NA_ROWS - 1, GRID_W, NA_HG * GRID_W)


def _merge_kernel(x_ref, g_ref, ya_ref, yb_ref, yc_ref, wg_ref, wb_ref, wo_ref, o_ref):
    x = x_ref[0]
    h = _rms(x, g_ref[...]).astype(BF16)
    merged = None
    for i, y_ref in enumerate((ya_ref, yb_ref, yc_ref)):
        gate = _sigmoid(_dot(h, wg_ref[:, i * D_MODEL:(i + 1) * D_MODEL]))
        t = gate * _dot(y_ref[0], wb_ref[i])
        merged = t if merged is None else merged + t
    o_ref[0] = x + _dot(merged.astype(BF16), wo_ref[...])


def _merge(x, g, ya, yb, yc, wg, wb, wo, tm):
    B, L, _ = x.shape
    row = lambda b, i: (b, i, 0)
    y_spec = pl.BlockSpec((1, tm, BRANCH_W), row)
    return pl.pallas_call(
        _merge_kernel,
        grid=(B, L // tm),
        in_specs=[pl.BlockSpec((1, tm, D_MODEL), row), _resident((1, D_MODEL)), y_spec, y_spec, y_spec,
                  _resident(wg.shape), _resident(wb.shape), _resident(wo.shape)],
        out_specs=pl.BlockSpec((1, tm, D_MODEL), row),
        out_shape=jax.ShapeDtypeStruct(x.shape, F32),
        compiler_params=_params(2), name="merge_out",
    )(x, g, ya, yb, yc, wg, wb, wo)


def _memkv_kernel(m_ref, g_ref, w_ref, o_ref):
    h = _rms(m_ref[0], g_ref[...]).astype(BF16)
    for c in range(0, 2 * D_MODEL, 2 * MXU_N):
        o_ref[0, :, c:c + 2 * MXU_N] = _dot(h, w_ref[:, c:c + 2 * MXU_N]).astype(o_ref.dtype)


def _memkv(mem, g, w):
    B, M, _ = mem.shape
    return pl.pallas_call(
        _memkv_kernel,
        grid=(B,),
        in_specs=[pl.BlockSpec((1, M, D_MODEL), lambda b: (b, 0, 0)), _resident((1, D_MODEL)),
                  _resident(w.shape)],
        out_specs=pl.BlockSpec((1, M, 2 * D_MODEL), lambda b: (b, 0, 0)),
        out_shape=jax.ShapeDtypeStruct((B, M, 2 * D_MODEL), BF16),
        compiler_params=_params(1), name="mem_kv",
    )(mem, g, w)


def _xattn_kernel(x_ref, g_ref, wq_ref, kv_ref, wo_ref, o_ref):
    x = x_ref[0]
    q = _dot(_rms(x, g_ref[...]).astype(BF16), wq_ref[...]).astype(BF16)
    cols = [slice(h * X_HEAD_DIM, (h + 1) * X_HEAD_DIM) for h in range(X_HEADS)]
    scores = [_dot_nt(q[:, c], kv_ref[0, :, c]) * (X_HEAD_DIM ** -0.5) for c in cols]
    probs = [jnp.exp(s - jnp.max(s, axis=-1, keepdims=True)) for s in scores]
    dens = [jnp.sum(p, axis=-1, keepdims=True) for p in probs]
    outs = [_dot(p.astype(BF16), kv_ref[0, :, D_MODEL + c.start:D_MODEL + c.stop])
            for p, c in zip(probs, cols)]
    attn = jnp.concatenate([(o / d).astype(BF16) for o, d in zip(outs, dens)], axis=1)
    o_ref[0] = x + _dot(attn, wo_ref[...])


def _xattn(x, g, wq, kv, wo, tm):
    B, L, _ = x.shape
    row = lambda b, i: (b, i, 0)
    return pl.pallas_call(
        _xattn_kernel,
        grid=(B, L // tm),
        in_specs=[pl.BlockSpec((1, tm, D_MODEL), row), _resident((1, D_MODEL)), _resident(wq.shape),
                  pl.BlockSpec((1, MEM_LEN, 2 * D_MODEL), lambda b, i: (b, 0, 0)),
                  _resident(wo.shape)],
        out_specs=pl.BlockSpec((1, tm, D_MODEL), row),
        out_shape=jax.ShapeDtypeStruct(x.shape, F32),
        compiler_params=_params(2), name="mem_xattn",
    )(x, g, wq, kv, wo)


def _ffn_kernel(x_ref, g_ref, wgu_ref, wd_ref, gf_ref, o_ref, act_ref, *, final_norm):
    x = x_ref[0]
    h = _rms(x, g_ref[...]).astype(BF16)
    for c in range(0, D_FF, MXU_N):
        gate = _dot(h, wgu_ref[:, c:c + MXU_N])
        up = _dot(h, wgu_ref[:, D_FF + c:D_FF + c + MXU_N])
        act_ref[:, c:c + MXU_N] = (gate * _sigmoid(gate) * up).astype(BF16)
    y = x + _dot(act_ref[...], wd_ref[...])
    if final_norm:
        y = _rms(y, gf_ref[...])
    o_ref[0] = y


def _ffn(x, g, wgu, wd, g_final, final_norm, tm):
    B, L, _ = x.shape
    row = lambda b, i: (b, i, 0)
    return pl.pallas_call(
        functools.partial(_ffn_kernel, final_norm=final_norm),
        grid=(B, L // tm),
        in_specs=[pl.BlockSpec((1, tm, D_MODEL), row), _resident((1, D_MODEL)),
                  _resident(wgu.shape), _resident(wd.shape), _resident((1, D_MODEL))],
        out_specs=pl.BlockSpec((1, tm, D_MODEL), row),
        out_shape=jax.ShapeDtypeStruct(x.shape, F32),
        scratch_shapes=[pltpu.VMEM((tm, D_FF), BF16)],
        compiler_params=_params(2), name="swiglu",
    )(x, g, wgu, wd, g_final)


TM = 1024


def _trunk(x, mem, layers, g_final):
    for l, p in enumerate(layers):
        pa, pr, pn = _proj(x, p["g_mix"], p["w_in"], TM)
        ya = _wattn(pa, p["a_bias"], p["a_sink"])
        yb = _retention(pr, p["r_lg"], p["r_gchunk"])
        yc = _nattn(pn, p["n_bias"])
        x = _merge(x, p["g_mix"], ya, yb, yc, p["w_gate"], p["w_branch"], p["w_mix_out"], TM)
        kv = _memkv(mem, p["g_mem"], p["w_xkv"])
        x = _xattn(x, p["g_xattn"], p["w_xq"], kv, p["w_xo"], TM)
        x = _ffn(x, p["g_ffn"], p["w_gate_up"], p["w_down"], g_final, l == DEPTH - 1, TM)
    return x


def kernel(x_prompt, x_sample, mem_prompt, mem_sample, g_mix, w_in, attn_sink, ret_decay, na_rpb,
           w_branch, w_mix_out, g_xattn, g_mem, w_xq, w_xkv, w_xo, g_ffn, w_gate_up, w_down, g_final):
    col_scale = np.ones((W_MIX,), np.float32)
    col_scale[A_Q:A_Q + A_KV] = SCORE_SCALE
    col_scale[W_A + W_R:W_A + W_R + N_W] = SCORE_SCALE
    col_scale = jnp.asarray(col_scale)
    layers = []
    for l in range(DEPTH):
        a_bias, a_sink = _wattn_tables(attn_sink[l])
        lg = jax.nn.log_sigmoid(ret_decay[l].astype(F32))
        layers.append(dict(
            g_mix=g_mix[l].reshape(1, D_MODEL),
            w_in=(w_in[l, :, :W_MIX] * col_scale).astype(BF16), w_gate=w_in[l, :, W_MIX:].astype(BF16),
            a_bias=a_bias, a_sink=a_sink,
            r_lg=lg, r_gchunk=jnp.exp(R_CHUNK * lg),
            n_bias=_na_bias(na_rpb[l]),
            w_branch=w_branch[l].astype(BF16), w_mix_out=w_mix_out[l].astype(BF16),
            g_xattn=g_xattn[l].reshape(1, D_MODEL), g_mem=g_mem[l].reshape(1, D_MODEL),
            w_xq=w_xq[l].astype(BF16), w_xkv=w_xkv[l].astype(BF16), w_xo=w_xo[l].astype(BF16),
            g_ffn=g_ffn[l].reshape(1, D_MODEL),
            w_gate_up=w_gate_up[l].astype(BF16), w_down=w_down[l].astype(BF16)))
    gf = g_final.reshape(1, D_MODEL)
    return (_trunk(x_prompt, mem_prompt, layers, gf), _trunk(x_sample, mem_sample, layers, gf))
```

```python
import functools

import numpy as np
import jax
import jax.numpy as jnp
from jax import lax
from jax.experimental import pallas as pl
from jax.experimental.pallas import tpu as pltpu

F32 = jnp.float32
BF16 = jnp.bfloat16

D_MODEL = 1024
DEPTH = 2
EPS = 1e-6
HEAD_DIM = 64
A_HEADS = 8
A_KV_HEADS = 2
A_GROUPS = A_HEADS // A_KV_HEADS
A_WINDOW = 128
A_BLOCK = 128
R_HEADS = 4
R_DK = 128
R_DV = 128
R_CHUNK = 128
N_HEADS = 8
GRID_W = 64
NA_ROWS = 8
NA_COLS = 16
MEM_LEN = 256
X_HEADS = 4
X_HEAD_DIM = D_MODEL // X_HEADS
D_FF = 2816
A_Q = A_HEADS * HEAD_DIM
A_KV = A_KV_HEADS * HEAD_DIM
R_W = R_HEADS * R_DK
N_W = N_HEADS * HEAD_DIM
BRANCH_W = 512
assert A_Q == R_W == N_W == BRANCH_W
W_A = A_Q + 2 * A_KV
W_R = 4 * R_W
W_N = 3 * N_W
W_MIX = W_A + W_R + W_N
NEG = -1e30
LOG2E = 1.4426950408889634
SCORE_SCALE = LOG2E * HEAD_DIM ** -0.5

MXU_N = 256
VMEM_LIMIT = 56 * 1024 * 1024


def _params(n_axes):
    return pltpu.CompilerParams(
        dimension_semantics=("arbitrary",) * n_axes, vmem_limit_bytes=VMEM_LIMIT)


def _resident(shape):
    return pl.BlockSpec(shape, lambda *_: (0,) * len(shape), pipeline_mode=pl.Buffered(1))


def _rms(x, g):
    return x * lax.rsqrt(jnp.mean(x * x, axis=-1, keepdims=True) + EPS) * g


def _sigmoid(x):
    return 1.0 / (1.0 + jnp.exp(-x))


def _dot(a, b):
    return jnp.dot(a, b, preferred_element_type=F32)


def _dot_nt(a, b):
    return lax.dot_general(a, b, (((1,), (1,)), ((), ())), preferred_element_type=F32)


def _dot_tn(a, b):
    return lax.dot_general(a, b, (((0,), (0,)), ((), ())), preferred_element_type=F32)


def _proj_kernel(x_ref, g_ref, w_ref, oa_ref, or_ref, on_ref):
    h = _rms(x_ref[0], g_ref[...]).astype(BF16)
    col = 0
    for o_ref, width in ((oa_ref, W_A), (or_ref, W_R), (on_ref, W_N)):
        for c in range(0, width, 2 * MXU_N):
            n = min(2 * MXU_N, width - c)
            o_ref[0, :, c:c + n] = _dot(h, w_ref[:, col + c:col + c + n]).astype(o_ref.dtype)
        col += width


def _proj(x, g, w, tm):
    B, L, _ = x.shape
    row = lambda b, i: (b, i, 0)
    return pl.pallas_call(
        _proj_kernel,
        grid=(B, L // tm),
        in_specs=[pl.BlockSpec((1, tm, D_MODEL), row), _resident((1, D_MODEL)),
                  _resident((D_MODEL, W_MIX))],
        out_specs=[pl.BlockSpec((1, tm, W_A), row), pl.BlockSpec((1, tm, W_R), row),
                   pl.BlockSpec((1, tm, W_N), row)],
        out_shape=[jax.ShapeDtypeStruct((B, L, W_A), BF16), jax.ShapeDtypeStruct((B, L, W_R), F32),
                   jax.ShapeDtypeStruct((B, L, W_N), BF16)],
        compiler_params=_params(2), name="proj_in",
    )(x, g, w)


WA_QB = 16
WA_STAGE = 4


def _wattn_kernel(q_ref, kp_ref, kc_ref, kn_ref, vp_ref, vc_ref, vn_ref, bias_ref, sink_ref, ones_ref,
                  o_ref, *, seq_len):
    n = pl.program_id(1)
    win = 3 * A_BLOCK
    k = jnp.concatenate([kp_ref[0], kc_ref[0], kn_ref[0]], axis=0).astype(F32)
    v = jnp.concatenate([vp_ref[0], vc_ref[0], vn_ref[0]], axis=0).astype(F32)
    lo = lax.broadcasted_iota(jnp.int32, k.shape, 1) < HEAD_DIM
    swap = lambda x: jnp.concatenate([x[:, HEAD_DIM:], x[:, :HEAD_DIM]], axis=1)
    ks, vs = swap(k), swap(v)
    k_lo = [jnp.where(lo, k, 0.0).astype(BF16), jnp.where(lo, ks, 0.0).astype(BF16)]
    k_hi = [jnp.where(lo, 0.0, ks).astype(BF16), jnp.where(lo, 0.0, k).astype(BF16)]
    v_lo = [jnp.where(lo, v, 0.0).astype(BF16), jnp.where(lo, vs, 0.0).astype(BF16)]
    v_hi = [jnp.where(lo, 0.0, vs).astype(BF16), jnp.where(lo, 0.0, v).astype(BF16)]
    j = lax.broadcasted_iota(jnp.int32, (win, 1), 0)
    even_rows = lax.broadcasted_iota(jnp.int32, (A_KV, 2 * A_BLOCK), 0) < HEAD_DIM
    units = [(i, h) for i in range(WA_QB) for h in range(A_KV_HEADS)]
    wrows = lambda i: slice(i * A_BLOCK, i * A_BLOCK + win)

    def scores(i, h):
        rows = slice(i * A_BLOCK, (i + 1) * A_BLOCK)
        c0 = h * A_GROUPS * HEAD_DIM
        qh = jnp.concatenate([q_ref[0, rows, c0:c0 + 2 * HEAD_DIM],
                              q_ref[0, rows, c0 + 2 * HEAD_DIM:c0 + 4 * HEAD_DIM]], axis=0)
        return [_dot_nt(kw[wrows(i)], qh) for kw in (k_lo[h], k_hi[h])]

    def softmax(i, h, par, s):
        s = s + bias_ref[h, par]
        if i in (0, WA_QB - 1):
            kpos = (n * WA_QB + i - 1) * A_BLOCK + j
            s = jnp.where((kpos >= 0) & (kpos < seq_len), s, NEG)
        sk = sink_ref[h, par]
        m = jnp.maximum(jnp.max(s, axis=0, keepdims=True), sk)
        return jnp.exp2(s - m).astype(BF16), jnp.exp2(sk - m)

    for g0 in range(0, len(units), WA_STAGE):
        grp = units[g0:g0 + WA_STAGE]
        all_s = [scores(i, h) for i, h in grp]
        all_p, all_e = [], []
        for (i, h), ss in zip(grp, all_s):
            (pe, ee), (po, eo) = [softmax(i, h, par, s) for par, s in enumerate(ss)]
            all_p.append(jnp.concatenate([pe, po], axis=0))
            all_e.append(jnp.where(even_rows, ee, eo))
        all_o = [_dot_tn(p, jnp.concatenate(
                     [jnp.concatenate([v_lo[h][wrows(i)], v_hi[h][wrows(i)]], axis=0), ones_ref[...]], axis=1))
                 for (i, h), p in zip(grp, all_p)]
        for (i, h), res, e in zip(grp, all_o, all_e):
            res = res[:, :A_KV] * (1.0 / (res[:, A_KV:] + jnp.transpose(e)))
            c0 = h * A_GROUPS * HEAD_DIM
            o_ref[0, i * A_BLOCK:(i + 1) * A_BLOCK, c0:c0 + 4 * HEAD_DIM] = jnp.concatenate(
                [res[:A_BLOCK], res[A_BLOCK:]], axis=1).astype(o_ref.dtype)


def _wattn(pa, bias, sink):
    B, L, _ = pa.shape
    ones = np.zeros((6 * A_BLOCK, A_KV), np.float32)
    ones[:3 * A_BLOCK, :HEAD_DIM] = 1.0
    ones[3 * A_BLOCK:, HEAD_DIM:] = 1.0
    ones = jnp.asarray(ones, BF16)
    nb = L // A_BLOCK
    assert nb % WA_QB == 0
    kcol, vcol = A_Q // A_KV, A_Q // A_KV + 1
    prev = lambda c: pl.BlockSpec((1, A_BLOCK, A_KV), lambda b, n: (b, jnp.maximum(n * WA_QB - 1, 0), c))
    cur = lambda c: pl.BlockSpec((1, WA_QB * A_BLOCK, A_KV), lambda b, n: (b, n, c))
    nxt = lambda c: pl.BlockSpec((1, A_BLOCK, A_KV),
                                 lambda b, n: (b, jnp.minimum((n + 1) * WA_QB, nb - 1), c))
    return pl.pallas_call(
        functools.partial(_wattn_kernel, seq_len=L),
        grid=(B, nb // WA_QB),
        in_specs=[pl.BlockSpec((1, WA_QB * A_BLOCK, A_Q), lambda b, n: (b, n, 0)),
                  prev(kcol), cur(kcol), nxt(kcol), prev(vcol), cur(vcol), nxt(vcol),
                  _resident(bias.shape), _resident(sink.shape), _resident(ones.shape)],
        out_specs=pl.BlockSpec((1, WA_QB * A_BLOCK, A_Q), lambda b, n: (b, n, 0)),
        out_shape=jax.ShapeDtypeStruct((B, L, A_Q), BF16),
        compiler_params=_params(2), name="window_gqa",
    )(pa, pa, pa, pa, pa, pa, pa, bias, sink, ones)


def _wattn_tables(attn_sink):
    i = np.arange(A_BLOCK)[None, :]
    j = np.arange(3 * A_BLOCK)[:, None]
    dist = np.abs(A_BLOCK + i - j)
    slopes = jnp.exp2(-8.0 * jnp.arange(1, A_HEADS + 1, dtype=F32) / A_HEADS)
    bias = -slopes[:, None, None] * jnp.asarray(dist, F32)[None]
    bias = jnp.where(jnp.asarray(dist <= A_WINDOW)[None], bias * LOG2E, NEG)
    bias = bias.reshape(A_KV_HEADS, 2, 2, 3 * A_BLOCK, A_BLOCK).transpose(0, 2, 3, 1, 4)
    bias = bias.reshape(A_KV_HEADS, 2, 3 * A_BLOCK, 2 * A_BLOCK)
    sink = jnp.repeat(attn_sink.astype(F32) * LOG2E, A_BLOCK).reshape(A_KV_HEADS, 2, 2, A_BLOCK)
    sink = sink.transpose(0, 2, 1, 3).reshape(A_KV_HEADS, 2, 1, 2 * A_BLOCK)
    return bias, sink


RET_UNROLL = 32


def _ret_kernel(lg_ref, gc_ref, q_ref, k_ref, v_ref, g_ref, o_ref, st_ref, *, nc):
    h = pl.program_id(1)
    C = R_CHUNK
    lgf, lgb = lg_ref[0, h], lg_ref[1, h]
    gcf, gcb = gc_ref[0, h], gc_ref[1, h]
    diff = (lax.broadcasted_iota(jnp.int32, (C, C), 0)
            - lax.broadcasted_iota(jnp.int32, (C, C), 1)).astype(F32)
    decay = (jnp.where(diff >= 0, jnp.exp(jnp.maximum(diff, 0.0) * lgf), 0.0)
             + jnp.where(diff <= 0, jnp.exp(jnp.maximum(-diff, 0.0) * lgb), 0.0))
    idx = lax.broadcasted_iota(jnp.int32, (C, R_DK), 0).astype(F32)
    wq2 = jnp.concatenate([jnp.exp((idx + 1.0) * lgf), jnp.exp((C - idx) * lgb)], axis=1)
    wk2 = jnp.concatenate([jnp.exp((C - 1.0 - idx) * lgf), jnp.exp(idx * lgb)], axis=1)

    def rows(n):
        return pl.ds(pl.multiple_of(n * C, C), C)

    def kv_body(i, carry):
        for u in range(RET_UNROLL):
            n = i * RET_UNROLL + u
            k = k_ref[0, rows(n), :] * (R_DK ** -0.5)
            kw = (jnp.concatenate([k, k], axis=1) * wk2).astype(BF16)
            st_ref[n] = _dot_tn(kw, v_ref[0, rows(n), :].astype(BF16))
        return carry

    lax.fori_loop(0, nc // RET_UNROLL, kv_body, 0)

    def scan_body(t, carry):
        sf, sb = carry
        nb = nc - 1 - t
        kvf = st_ref[t, :R_DK, :]
        st_ref[t, :R_DK, :] = sf
        kvb = st_ref[nb, R_DK:, :]
        st_ref[nb, R_DK:, :] = sb
        return sf * gcf + kvf, sb * gcb + kvb

    zero = jnp.zeros((R_DK, R_DV), F32)
    lax.fori_loop(0, nc, scan_body, (zero, zero))

    def out_body(i, carry):
        ns = [i * RET_UNROLL + u for u in range(RET_UNROLL)]
        qs = [q_ref[0, rows(n), :] for n in ns]
        inner = [_dot_nt(q.astype(BF16), (k_ref[0, rows(n), :] * (R_DK ** -0.5)).astype(BF16))
                 for n, q in zip(ns, qs)]
        lhs = [jnp.concatenate([s * decay, jnp.concatenate([q, q], axis=1) * wq2], axis=1).astype(BF16)
               for s, q in zip(inner, qs)]
        outs = [_dot(a, jnp.concatenate([v_ref[0, rows(n), :].astype(BF16),
                                         st_ref[n].astype(BF16)], axis=0))
                for n, a in zip(ns, lhs)]
        for n, o in zip(ns, outs):
            o = o * lax.rsqrt(jnp.mean(o * o, axis=-1, keepdims=True) + EPS)
            g = g_ref[0, rows(n), :]
            o_ref[0, rows(n), :] = (g * _sigmoid(g) * o).astype(o_ref.dtype)
        return carry

    lax.fori_loop(0, nc // RET_UNROLL, out_body, 0)


def _retention(pr, lg, gchunk):
    B, L, _ = pr.shape
    nc = L // R_CHUNK
    assert nc % RET_UNROLL == 0
    part = lambda p: pl.BlockSpec((1, L, R_DK), lambda b, h: (b, 0, p * R_HEADS + h))
    smem = pl.BlockSpec(memory_space=pltpu.SMEM)
    return pl.pallas_call(
        functools.partial(_ret_kernel, nc=nc),
        grid=(B, R_HEADS),
        in_specs=[smem, smem, part(0), part(1), part(2), part(3)],
        out_specs=pl.BlockSpec((1, L, R_DV), lambda b, h: (b, 0, h)),
        out_shape=jax.ShapeDtypeStruct((B, L, R_W), BF16),
        scratch_shapes=[pltpu.VMEM((nc, 2 * R_DK, R_DV), F32)],
        compiler_params=_params(2), name="retention",
    )(lg, gchunk, pr, pr, pr, pr)


NA_HG = 4
NA_UNROLL = 16


def _na_kernel(q_ref, k_ref, v_ref, bias_ref, o_ref, *, n_rows):
    keys = NA_ROWS * GRID_W
    head = lax.broadcasted_iota(jnp.int32, (GRID_W, NA_HG * HEAD_DIM), 1) // HEAD_DIM

    def qrows(r):
        return pl.ds(pl.multiple_of(r * GRID_W, GRID_W), GRID_W)

    def krows(r):
        rs = jnp.clip(r - NA_ROWS // 2, 0, n_rows - NA_ROWS)
        return pl.ds(pl.multiple_of(rs * GRID_W, GRID_W), keys)

    def scores(r):
        q = q_ref[0, qrows(r), :].astype(F32)
        qbd = jnp.concatenate([jnp.where(head == h, q, 0.0) for h in range(NA_HG)], axis=0).astype(BF16)
        return _dot_nt(k_ref[0, krows(r), :], qbd)

    def softmax(r, s):
        rs = jnp.clip(r - NA_ROWS // 2, 0, n_rows - NA_ROWS)
        s = s + jnp.concatenate(
            [bias_ref[0, rs - r + (NA_ROWS - 1) + j] for j in range(NA_ROWS)], axis=0)
        p = jnp.exp2(s - jnp.max(s, axis=0, keepdims=True))
        return (p * (1.0 / jnp.sum(p, axis=0, keepdims=True))).astype(BF16)

    def body(i, carry):
        rs_ = [i * NA_UNROLL + u for u in range(NA_UNROLL)]
        all_s = [scores(r) for r in rs_]
        all_p = [softmax(r, s) for r, s in zip(rs_, all_s)]
        all_o = [_dot_tn(p, v_ref[0, krows(r), :]) for r, p in zip(rs_, all_p)]
        for r, o in zip(rs_, all_o):
            out = o[:GRID_W]
            for h in range(1, NA_HG):
                out = jnp.where(head == h, o[h * GRID_W:(h + 1) * GRID_W], out)
            o_ref[0, qrows(r), :] = out.astype(o_ref.dtype)
        return carry

    lax.fori_loop(0, n_rows // NA_UNROLL, body, 0)


def _nattn(pn, bias):
    B, L, _ = pn.shape
    groups = N_HEADS // NA_HG
    assert (L // GRID_W) % NA_UNROLL == 0
    width = NA_HG * HEAD_DIM
    part = lambda p: pl.BlockSpec((1, L, width), lambda b, g: (b, 0, p * groups + g))
    return pl.pallas_call(
        functools.partial(_na_kernel, n_rows=L // GRID_W),
        grid=(B, groups),
        in_specs=[part(0), part(1), part(2),
                  pl.BlockSpec((1,) + bias.shape[1:], lambda b, g: (g, 0, 0, 0))],
        out_specs=pl.BlockSpec((1, L, width), lambda b, g: (b, 0, g)),
        out_shape=jax.ShapeDtypeStruct((B, L, N_W), BF16),
        compiler_params=_params(2), name="neighbourhood_attn",
    )(pn, pn, pn, bias)


def _na_bias(rpb):
    c = np.arange(GRID_W)
    cs = np.clip(c - NA_COLS // 2, 0, GRID_W - NA_COLS)
    valid = (c[:, None] >= cs[None, :]) & (c[:, None] < cs[None, :] + NA_COLS)
    a = rpb.astype(F32)
    period = 2 * GRID_W
    z = jnp.concatenate([a[..., NA_COLS - 1::-1],
                         jnp.full(a.shape[:-1] + (period - (2 * NA_COLS - 1),), NEG, F32),
                         a[..., :NA_COLS - 1:-1]], axis=-1)
    t = jnp.tile(z, GRID_W)[..., :GRID_W * (period - 1)]
    t = t.reshape(a.shape[:-1] + (GRID_W, period - 1))[..., :GRID_W]
    t = jnp.where(jnp.asarray(valid)[None, None], t * LOG2E, NEG)
    t = t.reshape(N_HEADS // NA_HG, NA_HG, 2 * NA_ROWS - 1, GRID_W, GRID_W)
    return t.transpose(0, 2, 3, 1, 4).reshape(N_HEADS // NA_HG, 2 * NA_ROWS - 1, GRID_W, NA_HG * GRID_W)


def _merge_kernel(x_ref, g_ref, ya_ref, yb_ref, yc_ref, wg_ref, wb_ref, wo_ref, o_ref):
    x = x_ref[0]
    h = _rms(x, g_ref[...]).astype(BF16)
    merged = None
    for i, y_ref in enumerate((ya_ref, yb_ref, yc_ref)):
        gate = _sigmoid(_dot(h, wg_ref[:, i * D_MODEL:(i + 1) * D_MODEL]))
        t = gate * _dot(y_ref[0], wb_ref[i])
        merged = t if merged is None else merged + t
    o_ref[0] = x + _dot(merged.astype(BF16), wo_ref[...])


def _merge(x, g, ya, yb, yc, wg, wb, wo, tm):
    B, L, _ = x.shape
    row = lambda b, i: (b, i, 0)
    y_spec = pl.BlockSpec((1, tm, BRANCH_W), row)
    return pl.pallas_call(
        _merge_kernel,
        grid=(B, L // tm),
        in_specs=[pl.BlockSpec((1, tm, D_MODEL), row), _resident((1, D_MODEL)), y_spec, y_spec, y_spec,
                  _resident(wg.shape), _resident(wb.shape), _resident(wo.shape)],
        out_specs=pl.BlockSpec((1, tm, D_MODEL), row),
        out_shape=jax.ShapeDtypeStruct(x.shape, F32),
        compiler_params=_params(2), name="merge_out",
    )(x, g, ya, yb, yc, wg, wb, wo)


def _memkv_kernel(m_ref, g_ref, w_ref, o_ref):
    h = _rms(m_ref[0], g_ref[...]).astype(BF16)
    for c in range(0, 2 * D_MODEL, 2 * MXU_N):
        o_ref[0, :, c:c + 2 * MXU_N] = _dot(h, w_ref[:, c:c + 2 * MXU_N]).astype(o_ref.dtype)


def _memkv(mem, g, w):
    B, M, _ = mem.shape
    return pl.pallas_call(
        _memkv_kernel,
        grid=(B,),
        in_specs=[pl.BlockSpec((1, M, D_MODEL), lambda b: (b, 0, 0)), _resident((1, D_MODEL)),
                  _resident(w.shape)],
        out_specs=pl.BlockSpec((1, M, 2 * D_MODEL), lambda b: (b, 0, 0)),
        out_shape=jax.ShapeDtypeStruct((B, M, 2 * D_MODEL), BF16),
        compiler_params=_params(1), name="mem_kv",
    )(mem, g, w)


def _xattn_kernel(x_ref, g_ref, wq_ref, kv_ref, wo_ref, o_ref):
    x = x_ref[0]
    q = _dot(_rms(x, g_ref[...]).astype(BF16), wq_ref[...]).astype(BF16)
    cols = [slice(h * X_HEAD_DIM, (h + 1) * X_HEAD_DIM) for h in range(X_HEADS)]
    scores = [_dot_nt(q[:, c], kv_ref[0, :, c]) * (X_HEAD_DIM ** -0.5) for c in cols]
    probs = [jnp.exp(s - jnp.max(s, axis=-1, keepdims=True)) for s in scores]
    dens = [jnp.sum(p, axis=-1, keepdims=True) for p in probs]
    outs = [_dot(p.astype(BF16), kv_ref[0, :, D_MODEL + c.start:D_MODEL + c.stop])
            for p, c in zip(probs, cols)]
    attn = jnp.concatenate([(o / d).astype(BF16) for o, d in zip(outs, dens)], axis=1)
    o_ref[0] = x + _dot(attn, wo_ref[...])


def _xattn(x, g, wq, kv, wo, tm):
    B, L, _ = x.shape
    row = lambda b, i: (b, i, 0)
    return pl.pallas_call(
        _xattn_kernel,
        grid=(B, L // tm),
        in_specs=[pl.BlockSpec((1, tm, D_MODEL), row), _resident((1, D_MODEL)), _resident(wq.shape),
                  pl.BlockSpec((1, MEM_LEN, 2 * D_MODEL), lambda b, i: (b, 0, 0)),
                  _resident(wo.shape)],
        out_specs=pl.BlockSpec((1, tm, D_MODEL), row),
        out_shape=jax.ShapeDtypeStruct(x.shape, F32),
        compiler_params=_params(2), name="mem_xattn",
    )(x, g, wq, kv, wo)


def _ffn_kernel(x_ref, g_ref, wgu_ref, wd_ref, gf_ref, o_ref, act_ref, *, final_norm):
    x = x_ref[0]
    h = _rms(x, g_ref[...]).astype(BF16)
    for c in range(0, D_FF, MXU_N):
        gate = _dot(h, wgu_ref[:, c:c + MXU_N])
        up = _dot(h, wgu_ref[:, D_FF + c:D_FF + c + MXU_N])
        act_ref[:, c:c + MXU_N] = (gate * _sigmoid(gate) * up).astype(BF16)
    y = x + _dot(act_ref[...], wd_ref[...])
    if final_norm:
        y = _rms(y, gf_ref[...])
    o_ref[0] = y


def _ffn(x, g, wgu, wd, g_final, final_norm, tm):
    B, L, _ = x.shape
    row = lambda b, i: (b, i, 0)
    return pl.pallas_call(
        functools.partial(_ffn_kernel, final_norm=final_norm),
        grid=(B, L // tm),
        in_specs=[pl.BlockSpec((1, tm, D_MODEL), row), _resident((1, D_MODEL)),
                  _resident(wgu.shape), _resident(wd.shape), _resident((1, D_MODEL))],
        out_specs=pl.BlockSpec((1, tm, D_MODEL), row),
        out_shape=jax.ShapeDtypeStruct(x.shape, F32),
        scratch_shapes=[pltpu.VMEM((tm, D_FF), BF16)],
        compiler_params=_params(2), name="swiglu",
    )(x, g, wgu, wd, g_final)


TM = 1024


def _trunk(x, mem, layers, g_final):
    for l, p in enumerate(layers):
        pa, pr, pn = _proj(x, p["g_mix"], p["w_in"], TM)
        ya = _wattn(pa, p["a_bias"], p["a_sink"])
        yb = _retention(pr, p["r_lg"], p["r_gchunk"])
        yc = _nattn(pn, p["n_bias"])
        x = _merge(x, p["g_mix"], ya, yb, yc, p["w_gate"], p["w_branch"], p["w_mix_out"], TM)
        kv = _memkv(mem, p["g_mem"], p["w_xkv"])
        x = _xattn(x, p["g_xattn"], p["w_xq"], kv, p["w_xo"], TM)
        x = _ffn(x, p["g_ffn"], p["w_gate_up"], p["w_down"], g_final, l == DEPTH - 1, TM)
    return x


def kernel(x_prompt, x_sample, mem_prompt, mem_sample, g_mix, w_in, attn_sink, ret_decay, na_rpb,
           w_branch, w_mix_out, g_xattn, g_mem, w_xq, w_xkv, w_xo, g_ffn, w_gate_up, w_down, g_final):
    col_scale = np.ones((W_MIX,), np.float32)
    col_scale[A_Q:A_Q + A_KV] = SCORE_SCALE
    col_scale[W_A + W_R:W_A + W_R + N_W] = SCORE_SCALE
    col_scale = jnp.asarray(col_scale)
    layers = []
    for l in range(DEPTH):
        a_bias, a_sink = _wattn_tables(attn_sink[l])
        lg = jax.nn.log_sigmoid(ret_decay[l].astype(F32))
        layers.append(dict(
            g_mix=g_mix[l].reshape(1, D_MODEL),
            w_in=(w_in[l, :, :W_MIX] * col_scale).astype(BF16), w_gate=w_in[l, :, W_MIX:].astype(BF16),
            a_bias=a_bias, a_sink=a_sink,
            r_lg=lg, r_gchunk=jnp.exp(R_CHUNK * lg),
            n_bias=_na_bias(na_rpb[l]),
            w_branch=w_branch[l].astype(BF16), w_mix_out=w_mix_out[l].astype(BF16),
            g_xattn=g_xattn[l].reshape(1, D_MODEL), g_mem=g_mem[l].reshape(1, D_MODEL),
            w_xq=w_xq[l].astype(BF16), w_xkv=w_xkv[l].astype(BF16), w_xo=w_xo[l].astype(BF16),
            g_ffn=g_ffn[l].reshape(1, D_MODEL),
            w_gate_up=w_gate_up[l].astype(BF16), w_down=w_down[l].astype(BF16)))
    gf = g_final.reshape(1, D_MODEL)
    return (_trunk(x_prompt, mem_prompt, layers, gf), _trunk(x_sample, mem_sample, layers, gf))
```

```python
import functools

import numpy as np
import jax
import jax.numpy as jnp
from jax import lax
from jax.experimental import pallas as pl
from jax.experimental.pallas import tpu as pltpu

F32 = jnp.float32
BF16 = jnp.bfloat16

D_MODEL = 1024
DEPTH = 2
EPS = 1e-6
HEAD_DIM = 64
A_HEADS = 8
A_KV_HEADS = 2
A_GROUPS = A_HEADS // A_KV_HEADS
A_WINDOW = 128
A_BLOCK = 128
R_HEADS = 4
R_DK = 128
R_DV = 128
R_CHUNK = 128
N_HEADS = 8
GRID_W = 64
NA_ROWS = 8
NA_COLS = 16
MEM_LEN = 256
X_HEADS = 4
X_HEAD_DIM = D_MODEL // X_HEADS
D_FF = 2816
A_Q = A_HEADS * HEAD_DIM
A_KV = A_KV_HEADS * HEAD_DIM
R_W = R_HEADS * R_DK
N_W = N_HEADS * HEAD_DIM
BRANCH_W = 512
assert A_Q == R_W == N_W == BRANCH_W
W_A = A_Q + 2 * A_KV
W_R = 4 * R_W
W_N = 3 * N_W
W_MIX = W_A + W_R + W_N
NEG = -1e30
LOG2E = 1.4426950408889634
SCORE_SCALE = LOG2E * HEAD_DIM ** -0.5

MXU_N = 256
VMEM_LIMIT = 56 * 1024 * 1024


def _params(n_axes):
    return pltpu.CompilerParams(
        dimension_semantics=("arbitrary",) * n_axes, vmem_limit_bytes=VMEM_LIMIT)


def _resident(shape):
    return pl.BlockSpec(shape, lambda *_: (0,) * len(shape), pipeline_mode=pl.Buffered(1))


def _rms(x, g):
    return x * lax.rsqrt(jnp.mean(x * x, axis=-1, keepdims=True) + EPS) * g


def _sigmoid(x):
    return 1.0 / (1.0 + jnp.exp(-x))


def _dot(a, b):
    return jnp.dot(a, b, preferred_element_type=F32)


def _dot_nt(a, b):
    return lax.dot_general(a, b, (((1,), (1,)), ((), ())), preferred_element_type=F32)


def _dot_tn(a, b):
    return lax.dot_general(a, b, (((0,), (0,)), ((), ())), preferred_element_type=F32)


def _proj_kernel(x_ref, g_ref, w_ref, oa_ref, or_ref, on_ref):
    h = _rms(x_ref[0], g_ref[...]).astype(BF16)
    col = 0
    for o_ref, width in ((oa_ref, W_A), (or_ref, W_R), (on_ref, W_N)):
        for c in range(0, width, 2 * MXU_N):
            n = min(2 * MXU_N, width - c)
            o_ref[0, :, c:c + n] = _dot(h, w_ref[:, col + c:col + c + n]).astype(o_ref.dtype)
        col += width


def _proj(x, g, w, tm):
    B, L, _ = x.shape
    row = lambda b, i: (b, i, 0)
    return pl.pallas_call(
        _proj_kernel,
        grid=(B, L // tm),
        in_specs=[pl.BlockSpec((1, tm, D_MODEL), row), _resident((1, D_MODEL)),
                  _resident((D_MODEL, W_MIX))],
        out_specs=[pl.BlockSpec((1, tm, W_A), row), pl.BlockSpec((1, tm, W_R), row),
                   pl.BlockSpec((1, tm, W_N), row)],
        out_shape=[jax.ShapeDtypeStruct((B, L, W_A), BF16), jax.ShapeDtypeStruct((B, L, W_R), F32),
                   jax.ShapeDtypeStruct((B, L, W_N), BF16)],
        compiler_params=_params(2), name="proj_in",
    )(x, g, w)


WA_QB = 16
WA_STAGE = 4


def _wattn_kernel(q_ref, kp_ref, kc_ref, kn_ref, vp_ref, vc_ref, vn_ref, bias_ref, sink_ref, ones_ref,
                  o_ref, *, seq_len):
    n = pl.program_id(1)
    win = 3 * A_BLOCK
    k = jnp.concatenate([kp_ref[0], kc_ref[0], kn_ref[0]], axis=0).astype(F32)
    v = jnp.concatenate([vp_ref[0], vc_ref[0], vn_ref[0]], axis=0).astype(F32)
    lo = lax.broadcasted_iota(jnp.int32, k.shape, 1) < HEAD_DIM
    swap = lambda x: jnp.concatenate([x[:, HEAD_DIM:], x[:, :HEAD_DIM]], axis=1)
    ks, vs = swap(k), swap(v)
    k_lo = [jnp.where(lo, k, 0.0).astype(BF16), jnp.where(lo, ks, 0.0).astype(BF16)]
    k_hi = [jnp.where(lo, 0.0, ks).astype(BF16), jnp.where(lo, 0.0, k).astype(BF16)]
    v_lo = [jnp.where(lo, v, 0.0).astype(BF16), jnp.where(lo, vs, 0.0).astype(BF16)]
    v_hi = [jnp.where(lo, 0.0, vs).astype(BF16), jnp.where(lo, 0.0, v).astype(BF16)]
    j = lax.broadcasted_iota(jnp.int32, (win, 1), 0)
    even_rows = lax.broadcasted_iota(jnp.int32, (A_KV, 2 * A_BLOCK), 0) < HEAD_DIM
    units = [(i, h) for i in range(WA_QB) for h in range(A_KV_HEADS)]
    wrows = lambda i: slice(i * A_BLOCK, i * A_BLOCK + win)

    def scores(i, h):
        rows = slice(i * A_BLOCK, (i + 1) * A_BLOCK)
        c0 = h * A_GROUPS * HEAD_DIM
        qh = jnp.concatenate([q_ref[0, rows, c0:c0 + 2 * HEAD_DIM],
                              q_ref[0, rows, c0 + 2 * HEAD_DIM:c0 + 4 * HEAD_DIM]], axis=0)
        return [_dot_nt(kw[wrows(i)], qh) for kw in (k_lo[h], k_hi[h])]

    def softmax(i, h, par, s):
        s = s + bias_ref[h, par]
        if i in (0, WA_QB - 1):
            kpos = (n * WA_QB + i - 1) * A_BLOCK + j
            s = jnp.where((kpos >= 0) & (kpos < seq_len), s, NEG)
        sk = sink_ref[h, par]
        m = jnp.maximum(jnp.max(s, axis=0, keepdims=True), sk)
        return jnp.exp2(s - m).astype(BF16), jnp.exp2(sk - m)

    for g0 in range(0, len(units), WA_STAGE):
        grp = units[g0:g0 + WA_STAGE]
        all_s = [scores(i, h) for i, h in grp]
        all_p, all_e = [], []
        for (i, h), ss in zip(grp, all_s):
            (pe, ee), (po, eo) = [softmax(i, h, par, s) for par, s in enumerate(ss)]
            all_p.append(jnp.concatenate([pe, po], axis=0))
            all_e.append(jnp.where(even_rows, ee, eo))
        all_o = [_dot_tn(p, jnp.concatenate(
                     [jnp.concatenate([v_lo[h][wrows(i)], v_hi[h][wrows(i)]], axis=0), ones_ref[...]], axis=1))
                 for (i, h), p in zip(grp, all_p)]
        for (i, h), res, e in zip(grp, all_o, all_e):
            res = res[:, :A_KV] * (1.0 / (res[:, A_KV:] + jnp.transpose(e)))
            c0 = h * A_GROUPS * HEAD_DIM
            o_ref[0, i * A_BLOCK:(i + 1) * A_BLOCK, c0:c0 + 4 * HEAD_DIM] = jnp.concatenate(
                [res[:A_BLOCK], res[A_BLOCK:]], axis=1).astype(o_ref.dtype)


def _wattn(pa, bias, sink):
    B, L, _ = pa.shape
    ones = np.zeros((6 * A_BLOCK, A_KV), np.float32)
    ones[:3 * A_BLOCK, :HEAD_DIM] = 1.0
    ones[3 * A_BLOCK:, HEAD_DIM:] = 1.0
    ones = jnp.asarray(ones, BF16)
    nb = L // A_BLOCK
    assert nb % WA_QB == 0
    kcol, vcol = A_Q // A_KV, A_Q // A_KV + 1
    prev = lambda c: pl.BlockSpec((1, A_BLOCK, A_KV), lambda b, n: (b, jnp.maximum(n * WA_QB - 1, 0), c))
    cur = lambda c: pl.BlockSpec((1, WA_QB * A_BLOCK, A_KV), lambda b, n: (b, n, c))
    nxt = lambda c: pl.BlockSpec((1, A_BLOCK, A_KV),
                                 lambda b, n: (b, jnp.minimum((n + 1) * WA_QB, nb - 1), c))
    return pl.pallas_call(
        functools.partial(_wattn_kernel, seq_len=L),
        grid=(B, nb // WA_QB),
        in_specs=[pl.BlockSpec((1, WA_QB * A_BLOCK, A_Q), lambda b, n: (b, n, 0)),
                  prev(kcol), cur(kcol), nxt(kcol), prev(vcol), cur(vcol), nxt(vcol),
                  _resident(bias.shape), _resident(sink.shape), _resident(ones.shape)],
        out_specs=pl.BlockSpec((1, WA_QB * A_BLOCK, A_Q), lambda b, n: (b, n, 0)),
        out_shape=jax.ShapeDtypeStruct((B, L, A_Q), BF16),
        compiler_params=_params(2), name="window_gqa",
    )(pa, pa, pa, pa, pa, pa, pa, bias, sink, ones)


def _wattn_tables(attn_sink):
    i = np.arange(A_BLOCK)[None, :]
    j = np.arange(3 * A_BLOCK)[:, None]
    dist = np.abs(A_BLOCK + i - j)
    slopes = jnp.exp2(-8.0 * jnp.arange(1, A_HEADS + 1, dtype=F32) / A_HEADS)
    bias = -slopes[:, None, None] * jnp.asarray(dist, F32)[None]
    bias = jnp.where(jnp.asarray(dist <= A_WINDOW)[None], bias * LOG2E, NEG)
    bias = bias.reshape(A_KV_HEADS, 2, 2, 3 * A_BLOCK, A_BLOCK).transpose(0, 2, 3, 1, 4)
    bias = bias.reshape(A_KV_HEADS, 2, 3 * A_BLOCK, 2 * A_BLOCK)
    sink = jnp.repeat(attn_sink.astype(F32) * LOG2E, A_BLOCK).reshape(A_KV_HEADS, 2, 2, A_BLOCK)
    sink = sink.transpose(0, 2, 1, 3).reshape(A_KV_HEADS, 2, 1, 2 * A_BLOCK)
    return bias, sink


RET_UNROLL = 32


def _ret_kernel(lg_ref, gc_ref, q_ref, k_ref, v_ref, g_ref, o_ref, st_ref, *, nc):
    h = pl.program_id(1)
    C = R_CHUNK
    lgf, lgb = lg_ref[0, h], lg_ref[1, h]
    gcf, gcb = gc_ref[0, h], gc_ref[1, h]
    diff = (lax.broadcasted_iota(jnp.int32, (C, C), 0)
            - lax.broadcasted_iota(jnp.int32, (C, C), 1)).astype(F32)
    decay = (jnp.where(diff >= 0, jnp.exp(jnp.maximum(diff, 0.0) * lgf), 0.0)
             + jnp.where(diff <= 0, jnp.exp(jnp.maximum(-diff, 0.0) * lgb), 0.0))
    idx = lax.broadcasted_iota(jnp.int32, (C, R_DK), 0).astype(F32)
    wq2 = jnp.concatenate([jnp.exp((idx + 1.0) * lgf), jnp.exp((C - idx) * lgb)], axis=1)
    wk2 = jnp.concatenate([jnp.exp((C - 1.0 - idx) * lgf), jnp.exp(idx * lgb)], axis=1)

    def rows(n):
        return pl.ds(pl.multiple_of(n * C, C), C)

    def kv_body(i, carry):
        for u in range(RET_UNROLL):
            n = i * RET_UNROLL + u
            k = k_ref[0, rows(n), :] * (R_DK ** -0.5)
            kw = (jnp.concatenate([k, k], axis=1) * wk2).astype(BF16)
            st_ref[n] = _dot_tn(kw, v_ref[0, rows(n), :].astype(BF16))
        return carry

    lax.fori_loop(0, nc // RET_UNROLL, kv_body, 0)

    def scan_body(t, carry):
        sf, sb = carry
        nb = nc - 1 - t
        kvf = st_ref[t, :R_DK, :]
        st_ref[t, :R_DK, :] = sf
        kvb = st_ref[nb, R_DK:, :]
        st_ref[nb, R_DK:, :] = sb
        return sf * gcf + kvf, sb * gcb + kvb

    zero = jnp.zeros((R_DK, R_DV), F32)
    lax.fori_loop(0, nc, scan_body, (zero, zero))

    def out_body(i, carry):
        ns = [i * RET_UNROLL + u for u in range(RET_UNROLL)]
        qs = [q_ref[0, rows(n), :] for n in ns]
        inner = [_dot_nt(q.astype(BF16), (k_ref[0, rows(n), :] * (R_DK ** -0.5)).astype(BF16))
                 for n, q in zip(ns, qs)]
        lhs = [jnp.concatenate([s * decay, jnp.concatenate([q, q], axis=1) * wq2], axis=1).astype(BF16)
               for s, q in zip(inner, qs)]
        outs = [_dot(a, jnp.concatenate([v_ref[0, rows(n), :].astype(BF16),
                                         st_ref[n].astype(BF16)], axis=0))
                for n, a in zip(ns, lhs)]
        for n, o in zip(ns, outs):
            o = o * lax.rsqrt(jnp.mean(o * o, axis=-1, keepdims=True) + EPS)
            g = g_ref[0, rows(n), :]
            o_ref[0, rows(n), :] = (g * _sigmoid(g) * o).astype(o_ref.dtype)
        return carry

    lax.fori_loop(0, nc // RET_UNROLL, out_body, 0)


def _retention(pr, lg, gchunk):
    B, L, _ = pr.shape
    nc = L // R_CHUNK
    assert nc % RET_UNROLL == 0
    part = lambda p: pl.BlockSpec((1, L, R_DK), lambda b, h: (b, 0, p * R_HEADS + h))
    smem = pl.BlockSpec(memory_space=pltpu.SMEM)
    return pl.pallas_call(
        functools.partial(_ret_kernel, nc=nc),
        grid=(B, R_HEADS),
        in_specs=[smem, smem, part(0), part(1), part(2), part(3)],
        out_specs=pl.BlockSpec((1, L, R_DV), lambda b, h: (b, 0, h)),
        out_shape=jax.ShapeDtypeStruct((B, L, R_W), BF16),
        scratch_shapes=[pltpu.VMEM((nc, 2 * R_DK, R_DV), F32)],
        compiler_params=_params(2), name="retention",
    )(lg, gchunk, pr, pr, pr, pr)


NA_HG = 4
NA_UNROLL = 16


def _na_kernel(q_ref, k_ref, v_ref, bias_ref, o_ref, *, n_rows):
    keys = NA_ROWS * GRID_W
    head = lax.broadcasted_iota(jnp.int32, (GRID_W, NA_HG * HEAD_DIM), 1) // HEAD_DIM

    def qrows(r):
        return pl.ds(pl.multiple_of(r * GRID_W, GRID_W), GRID_W)

    def krows(r):
        rs = jnp.clip(r - NA_ROWS // 2, 0, n_rows - NA_ROWS)
        return pl.ds(pl.multiple_of(rs * GRID_W, GRID_W), keys)

    def scores(r):
        q = q_ref[0, qrows(r), :].astype(F32)
        qbd = jnp.concatenate([jnp.where(head == h, q, 0.0) for h in range(NA_HG)], axis=0).astype(BF16)
        return _dot_nt(k_ref[0, krows(r), :], qbd)

    def softmax(r, s):
        rs = jnp.clip(r - NA_ROWS // 2, 0, n_rows - NA_ROWS)
        s = s + jnp.concatenate(
            [bias_ref[0, rs - r + (NA_ROWS - 1) + j] for j in range(NA_ROWS)], axis=0)
        p = jnp.exp2(s - jnp.max(s, axis=0, keepdims=True))
        return p.astype(BF16), 1.0 / jnp.sum(p, axis=0, keepdims=True)

    def body(i, carry):
        rs_ = [i * NA_UNROLL + u for u in range(NA_UNROLL)]
        all_s = [scores(r) for r in rs_]
        all_p = [softmax(r, s) for r, s in zip(rs_, all_s)]
        all_o = [_dot_tn(p, v_ref[0, krows(r), :]) for r, (p, _) in zip(rs_, all_p)]
        for r, o, (_, inv) in zip(rs_, all_o, all_p):
            scale = jnp.transpose(jnp.broadcast_to(inv, (2 * HEAD_DIM, NA_HG * GRID_W)))
            o = o * jnp.concatenate([scale, scale], axis=1)
            out = o[:GRID_W]
            for h in range(1, NA_HG):
                out = jnp.where(head == h, o[h * GRID_W:(h + 1) * GRID_W], out)
            o_ref[0, qrows(r), :] = out.astype(o_ref.dtype)
        return carry

    lax.fori_loop(0, n_rows // NA_UNROLL, body, 0)


def _nattn(pn, bias):
    B, L, _ = pn.shape
    groups = N_HEADS // NA_HG
    assert (L // GRID_W) % NA_UNROLL == 0
    width = NA_HG * HEAD_DIM
    part = lambda p: pl.BlockSpec((1, L, width), lambda b, g: (b, 0, p * groups + g))
    return pl.pallas_call(
        functools.partial(_na_kernel, n_rows=L // GRID_W),
        grid=(B, groups),
        in_specs=[part(0), part(1), part(2),
                  pl.BlockSpec((1,) + bias.shape[1:], lambda b, g: (g, 0, 0, 0))],
        out_specs=pl.BlockSpec((1, L, width), lambda b, g: (b, 0, g)),
        out_shape=jax.ShapeDtypeStruct((B, L, N_W), BF16),
        compiler_params=_params(2), name="neighbourhood_attn",
    )(pn, pn, pn, bias)


def _na_bias(rpb):
    c = np.arange(GRID_W)
    cs = np.clip(c - NA_COLS // 2, 0, GRID_W - NA_COLS)
    valid = (c[:, None] >= cs[None, :]) & (c[:, None] < cs[None, :] + NA_COLS)
    a = rpb.astype(F32)
    period = 2 * GRID_W
    z = jnp.concatenate([a[..., NA_COLS - 1::-1],
                         jnp.full(a.shape[:-1] + (period - (2 * NA_COLS - 1),), NEG, F32),
                         a[..., :NA_COLS - 1:-1]], axis=-1)
    t = jnp.tile(z, GRID_W)[..., :GRID_W * (period - 1)]
    t = t.reshape(a.shape[:-1] + (GRID_W, period - 1))[..., :GRID_W]
    t = jnp.where(jnp.asarray(valid)[None, None], t * LOG2E, NEG)
    t = t.reshape(N_HEADS // NA_HG, NA_HG, 2 * NA_ROWS - 1, GRID_W, GRID_W)
    return t.transpose(0, 2, 3, 1, 4).reshape(N_HEADS // NA_HG, 2 * NA_ROWS - 1, GRID_W, NA_HG * GRID_W)


def _merge_kernel(x_ref, g_ref, ya_ref, yb_ref, yc_ref, wg_ref, wb_ref, wo_ref, o_ref):
    x = x_ref[0]
    h = _rms(x, g_ref[...]).astype(BF16)
    merged = None
    for i, y_ref in enumerate((ya_ref, yb_ref, yc_ref)):
        gate = _sigmoid(_dot(h, wg_ref[:, i * D_MODEL:(i + 1) * D_MODEL]))
        t = gate * _dot(y_ref[0], wb_ref[i])
        merged = t if merged is None else merged + t
    o_ref[0] = x + _dot(merged.astype(BF16), wo_ref[...])


def _merge(x, g, ya, yb, yc, wg, wb, wo, tm):
    B, L, _ = x.shape
    row = lambda b, i: (b, i, 0)
    y_spec = pl.BlockSpec((1, tm, BRANCH_W), row)
    return pl.pallas_call(
        _merge_kernel,
        grid=(B, L // tm),
        in_specs=[pl.BlockSpec((1, tm, D_MODEL), row), _resident((1, D_MODEL)), y_spec, y_spec, y_spec,
                  _resident(wg.shape), _resident(wb.shape), _resident(wo.shape)],
        out_specs=pl.BlockSpec((1, tm, D_MODEL), row),
        out_shape=jax.ShapeDtypeStruct(x.shape, F32),
        compiler_params=_params(2), name="merge_out",
    )(x, g, ya, yb, yc, wg, wb, wo)


def _xattn_kernel(x_ref, g_ref, wq_ref, mem_ref, gm_ref, wkv_ref, wo_ref, o_ref, kv_ref):
    @pl.when(pl.program_id(1) == 0)
    def _():
        hm = _rms(mem_ref[0], gm_ref[...]).astype(BF16)
        for c in range(0, 2 * D_MODEL, 2 * MXU_N):
            kv_ref[:, c:c + 2 * MXU_N] = _dot(hm, wkv_ref[:, c:c + 2 * MXU_N]).astype(BF16)

    x = x_ref[0]
    q = _dot(_rms(x, g_ref[...]).astype(BF16), wq_ref[...]).astype(BF16)
    cols = [slice(h * X_HEAD_DIM, (h + 1) * X_HEAD_DIM) for h in range(X_HEADS)]
    scores = [_dot_nt(q[:, c], kv_ref[:, c]) * (X_HEAD_DIM ** -0.5) for c in cols]
    probs = [jnp.exp(s - jnp.max(s, axis=-1, keepdims=True)) for s in scores]
    dens = [jnp.sum(p, axis=-1, keepdims=True) for p in probs]
    outs = [_dot(p.astype(BF16), kv_ref[:, D_MODEL + c.start:D_MODEL + c.stop])
            for p, c in zip(probs, cols)]
    attn = jnp.concatenate([(o / d).astype(BF16) for o, d in zip(outs, dens)], axis=1)
    o_ref[0] = x + _dot(attn, wo_ref[...])


def _xattn(x, g, wq, mem, g_mem, wkv, wo, tm):
    B, L, _ = x.shape
    row = lambda b, i: (b, i, 0)
    return pl.pallas_call(
        _xattn_kernel,
        grid=(B, L // tm),
        in_specs=[pl.BlockSpec((1, tm, D_MODEL), row), _resident((1, D_MODEL)), _resident(wq.shape),
                  pl.BlockSpec((1, MEM_LEN, D_MODEL), lambda b, i: (b, 0, 0)), _resident((1, D_MODEL)),
                  _resident(wkv.shape), _resident(wo.shape)],
        out_specs=pl.BlockSpec((1, tm, D_MODEL), row),
        out_shape=jax.ShapeDtypeStruct(x.shape, F32),
        scratch_shapes=[pltpu.VMEM((MEM_LEN, 2 * D_MODEL), BF16)],
        compiler_params=_params(2), name="mem_xattn",
    )(x, g, wq, mem, g_mem, wkv, wo)


def _ffn_kernel(x_ref, g_ref, wgu_ref, wd_ref, gf_ref, o_ref, act_ref, *, final_norm):
    x = x_ref[0]
    h = _rms(x, g_ref[...]).astype(BF16)
    for c in range(0, D_FF, MXU_N):
        gate = _dot(h, wgu_ref[:, c:c + MXU_N])
        up = _dot(h, wgu_ref[:, D_FF + c:D_FF + c + MXU_N])
        act_ref[:, c:c + MXU_N] = (gate * _sigmoid(gate) * up).astype(BF16)
    y = x + _dot(act_ref[...], wd_ref[...])
    if final_norm:
        y = _rms(y, gf_ref[...])
    o_ref[0] = y


def _ffn(x, g, wgu, wd, g_final, final_norm, tm):
    B, L, _ = x.shape
    row = lambda b, i: (b, i, 0)
    return pl.pallas_call(
        functools.partial(_ffn_kernel, final_norm=final_norm),
        grid=(B, L // tm),
        in_specs=[pl.BlockSpec((1, tm, D_MODEL), row), _resident((1, D_MODEL)),
                  _resident(wgu.shape), _resident(wd.shape), _resident((1, D_MODEL))],
        out_specs=pl.BlockSpec((1, tm, D_MODEL), row),
        out_shape=jax.ShapeDtypeStruct(x.shape, F32),
        scratch_shapes=[pltpu.VMEM((tm, D_FF), BF16)],
        compiler_params=_params(2), name="swiglu",
    )(x, g, wgu, wd, g_final)


TM = 1024


def _trunk(x, mem, layers, g_final):
    for l, p in enumerate(layers):
        pa, pr, pn = _proj(x, p["g_mix"], p["w_in"], TM)
        ya = _wattn(pa, p["a_bias"], p["a_sink"])
        yb = _retention(pr, p["r_lg"], p["r_gchunk"])
        yc = _nattn(pn, p["n_bias"])
        x = _merge(x, p["g_mix"], ya, yb, yc, p["w_gate"], p["w_branch"], p["w_mix_out"], TM)
        x = _xattn(x, p["g_xattn"], p["w_xq"], mem, p["g_mem"], p["w_xkv"], p["w_xo"], TM)
        x = _ffn(x, p["g_ffn"], p["w_gate_up"], p["w_down"], g_final, l == DEPTH - 1, TM)
    return x


def kernel(x_prompt, x_sample, mem_prompt, mem_sample, g_mix, w_in, attn_sink, ret_decay, na_rpb,
           w_branch, w_mix_out, g_xattn, g_mem, w_xq, w_xkv, w_xo, g_ffn, w_gate_up, w_down, g_final):
    col_scale = np.ones((W_MIX,), np.float32)
    col_scale[A_Q:A_Q + A_KV] = SCORE_SCALE
    col_scale[W_A + W_R:W_A + W_R + N_W] = SCORE_SCALE
    col_scale = jnp.asarray(col_scale)
    layers = []
    for l in range(DEPTH):
        a_bias, a_sink = _wattn_tables(attn_sink[l])
        lg = jax.nn.log_sigmoid(ret_decay[l].astype(F32))
        layers.append(dict(
            g_mix=g_mix[l].reshape(1, D_MODEL),
            w_in=(w_in[l, :, :W_MIX] * col_scale).astype(BF16), w_gate=w_in[l, :, W_MIX:].astype(BF16),
            a_bias=a_bias, a_sink=a_sink,
            r_lg=lg, r_gchunk=jnp.exp(R_CHUNK * lg),
            n_bias=_na_bias(na_rpb[l]),
            w_branch=w_branch[l].astype(BF16), w_mix_out=w_mix_out[l].astype(BF16),
            g_xattn=g_xattn[l].reshape(1, D_MODEL), g_mem=g_mem[l].reshape(1, D_MODEL),
            w_xq=w_xq[l].astype(BF16), w_xkv=w_xkv[l].astype(BF16), w_xo=w_xo[l].astype(BF16),
            g_ffn=g_ffn[l].reshape(1, D_MODEL),
            w_gate_up=w_gate_up[l].astype(BF16), w_down=w_down[l].astype(BF16)))
    gf = g_final.reshape(1, D_MODEL)
    return (_trunk(x_prompt, mem_prompt, layers, gf), _trunk(x_sample, mem_sample, layers, gf))
```

```python
import functools

import numpy as np
import jax
import jax.numpy as jnp
from jax import lax
from jax.experimental import pallas as pl
from jax.experimental.pallas import tpu as pltpu

F32 = jnp.float32
BF16 = jnp.bfloat16

D_MODEL = 1024
DEPTH = 2
EPS = 1e-6
HEAD_DIM = 64
A_HEADS = 8
A_KV_HEADS = 2
A_GROUPS = A_HEADS // A_KV_HEADS
A_WINDOW = 128
A_BLOCK = 128
R_HEADS = 4
R_DK = 128
R_DV = 128
R_CHUNK = 128
N_HEADS = 8
GRID_W = 64
NA_ROWS = 8
NA_COLS = 16
MEM_LEN = 256
X_HEADS = 4
X_HEAD_DIM = D_MODEL // X_HEADS
D_FF = 2816
A_Q = A_HEADS * HEAD_DIM
A_KV = A_KV_HEADS * HEAD_DIM
R_W = R_HEADS * R_DK
N_W = N_HEADS * HEAD_DIM
BRANCH_W = 512
assert A_Q == R_W == N_W == BRANCH_W
W_A = A_Q + 2 * A_KV
W_R = 4 * R_W
W_N = 3 * N_W
W_MIX = W_A + W_R + W_N
NEG = -1e30
LOG2E = 1.4426950408889634
SCORE_SCALE = LOG2E * HEAD_DIM ** -0.5

MXU_N = 256
VMEM_LIMIT = 56 * 1024 * 1024


def _params(n_axes):
    return pltpu.CompilerParams(
        dimension_semantics=("arbitrary",) * n_axes, vmem_limit_bytes=VMEM_LIMIT)


def _resident(shape):
    return pl.BlockSpec(shape, lambda *_: (0,) * len(shape), pipeline_mode=pl.Buffered(1))


def _rms(x, g):
    return x * lax.rsqrt(jnp.mean(x * x, axis=-1, keepdims=True) + EPS) * g


def _sigmoid(x):
    return 1.0 / (1.0 + jnp.exp(-x))


def _dot(a, b):
    return jnp.dot(a, b, preferred_element_type=F32)


def _dot_nt(a, b):
    return lax.dot_general(a, b, (((1,), (1,)), ((), ())), preferred_element_type=F32)


def _dot_tn(a, b):
    return lax.dot_general(a, b, (((0,), (0,)), ((), ())), preferred_element_type=F32)


def _proj_kernel(x_ref, g_ref, w_ref, oa_ref, or_ref, on_ref):
    h = _rms(x_ref[0], g_ref[...]).astype(BF16)
    col = 0
    for o_ref, width in ((oa_ref, W_A), (or_ref, W_R), (on_ref, W_N)):
        for c in range(0, width, 2 * MXU_N):
            n = min(2 * MXU_N, width - c)
            o_ref[0, :, c:c + n] = _dot(h, w_ref[:, col + c:col + c + n]).astype(o_ref.dtype)
        col += width


def _proj(x, g, w, tm):
    B, L, _ = x.shape
    row = lambda b, i: (b, i, 0)
    return pl.pallas_call(
        _proj_kernel,
        grid=(B, L // tm),
        in_specs=[pl.BlockSpec((1, tm, D_MODEL), row), _resident((1, D_MODEL)),
                  _resident((D_MODEL, W_MIX))],
        out_specs=[pl.BlockSpec((1, tm, W_A), row), pl.BlockSpec((1, tm, W_R), row),
                   pl.BlockSpec((1, tm, W_N), row)],
        out_shape=[jax.ShapeDtypeStruct((B, L, W_A), BF16), jax.ShapeDtypeStruct((B, L, W_R), F32),
                   jax.ShapeDtypeStruct((B, L, W_N), BF16)],
        compiler_params=_params(2), name="proj_in",
    )(x, g, w)


WA_QB = 16
WA_STAGE = 4


def _wattn_kernel(q_ref, kp_ref, kc_ref, kn_ref, vp_ref, vc_ref, vn_ref, bias_ref, sink_ref, ones_ref,
                  o_ref, *, seq_len):
    n = pl.program_id(1)
    win = 3 * A_BLOCK
    k = jnp.concatenate([kp_ref[0], kc_ref[0], kn_ref[0]], axis=0).astype(F32)
    v = jnp.concatenate([vp_ref[0], vc_ref[0], vn_ref[0]], axis=0).astype(F32)
    lo = lax.broadcasted_iota(jnp.int32, k.shape, 1) < HEAD_DIM
    swap = lambda x: jnp.concatenate([x[:, HEAD_DIM:], x[:, :HEAD_DIM]], axis=1)
    ks, vs = swap(k), swap(v)
    k_lo = [jnp.where(lo, k, 0.0).astype(BF16), jnp.where(lo, ks, 0.0).astype(BF16)]
    k_hi = [jnp.where(lo, 0.0, ks).astype(BF16), jnp.where(lo, 0.0, k).astype(BF16)]
    v_lo = [jnp.where(lo, v, 0.0).astype(BF16), jnp.where(lo, vs, 0.0).astype(BF16)]
    v_hi = [jnp.where(lo, 0.0, vs).astype(BF16), jnp.where(lo, 0.0, v).astype(BF16)]
    j = lax.broadcasted_iota(jnp.int32, (win, 1), 0)
    even_rows = lax.broadcasted_iota(jnp.int32, (A_KV, 2 * A_BLOCK), 0) < HEAD_DIM
    units = [(i, h) for i in range(WA_QB) for h in range(A_KV_HEADS)]
    wrows = lambda i: slice(i * A_BLOCK, i * A_BLOCK + win)

    def scores(i, h):
        rows = slice(i * A_BLOCK, (i + 1) * A_BLOCK)
        c0 = h * A_GROUPS * HEAD_DIM
        qh = jnp.concatenate([q_ref[0, rows, c0:c0 + 2 * HEAD_DIM],
                              q_ref[0, rows, c0 + 2 * HEAD_DIM:c0 + 4 * HEAD_DIM]], axis=0)
        return [_dot_nt(kw[wrows(i)], qh) for kw in (k_lo[h], k_hi[h])]

    def softmax(i, h, par, s):
        s = s + bias_ref[h, par]
        if i in (0, WA_QB - 1):
            kpos = (n * WA_QB + i - 1) * A_BLOCK + j
            s = jnp.where((kpos >= 0) & (kpos < seq_len), s, NEG)
        sk = sink_ref[h, par]
        m = jnp.maximum(jnp.max(s, axis=0, keepdims=True), sk)
        return jnp.exp2(s - m).astype(BF16), jnp.exp2(sk - m)

    for g0 in range(0, len(units), WA_STAGE):
        grp = units[g0:g0 + WA_STAGE]
        all_s = [scores(i, h) for i, h in grp]
        all_p, all_e = [], []
        for (i, h), ss in zip(grp, all_s):
            (pe, ee), (po, eo) = [softmax(i, h, par, s) for par, s in enumerate(ss)]
            all_p.append(jnp.concatenate([pe, po], axis=0))
            all_e.append(jnp.where(even_rows, ee, eo))
        all_o = [_dot_tn(p, jnp.concatenate(
                     [jnp.concatenate([v_lo[h][wrows(i)], v_hi[h][wrows(i)]], axis=0), ones_ref[...]], axis=1))
                 for (i, h), p in zip(grp, all_p)]
        for (i, h), res, e in zip(grp, all_o, all_e):
            res = res[:, :A_KV] * (1.0 / (res[:, A_KV:] + jnp.transpose(e)))
            c0 = h * A_GROUPS * HEAD_DIM
            o_ref[0, i * A_BLOCK:(i + 1) * A_BLOCK, c0:c0 + 4 * HEAD_DIM] = jnp.concatenate(
                [res[:A_BLOCK], res[A_BLOCK:]], axis=1).astype(o_ref.dtype)


def _wattn(pa, bias, sink):
    B, L, _ = pa.shape
    ones = np.zeros((6 * A_BLOCK, A_KV), np.float32)
    ones[:3 * A_BLOCK, :HEAD_DIM] = 1.0
    ones[3 * A_BLOCK:, HEAD_DIM:] = 1.0
    ones = jnp.asarray(ones, BF16)
    nb = L // A_BLOCK
    assert nb % WA_QB == 0
    kcol, vcol = A_Q // A_KV, A_Q // A_KV + 1
    prev = lambda c: pl.BlockSpec((1, A_BLOCK, A_KV), lambda b, n: (b, jnp.maximum(n * WA_QB - 1, 0), c))
    cur = lambda c: pl.BlockSpec((1, WA_QB * A_BLOCK, A_KV), lambda b, n: (b, n, c))
    nxt = lambda c: pl.BlockSpec((1, A_BLOCK, A_KV),
                                 lambda b, n: (b, jnp.minimum((n + 1) * WA_QB, nb - 1), c))
    return pl.pallas_call(
        functools.partial(_wattn_kernel, seq_len=L),
        grid=(B, nb // WA_QB),
        in_specs=[pl.BlockSpec((1, WA_QB * A_BLOCK, A_Q), lambda b, n: (b, n, 0)),
                  prev(kcol), cur(kcol), nxt(kcol), prev(vcol), cur(vcol), nxt(vcol),
                  _resident(bias.shape), _resident(sink.shape), _resident(ones.shape)],
        out_specs=pl.BlockSpec((1, WA_QB * A_BLOCK, A_Q), lambda b, n: (b, n, 0)),
        out_shape=jax.ShapeDtypeStruct((B, L, A_Q), BF16),
        compiler_params=_params(2), name="window_gqa",
    )(pa, pa, pa, pa, pa, pa, pa, bias, sink, ones)


def _wattn_tables(attn_sink):
    i = np.arange(A_BLOCK)[None, :]
    j = np.arange(3 * A_BLOCK)[:, None]
    dist = np.abs(A_BLOCK + i - j)
    slopes = jnp.exp2(-8.0 * jnp.arange(1, A_HEADS + 1, dtype=F32) / A_HEADS)
    bias = -slopes[:, None, None] * jnp.asarray(dist, F32)[None]
    bias = jnp.where(jnp.asarray(dist <= A_WINDOW)[None], bias * LOG2E, NEG)
    bias = bias.reshape(A_KV_HEADS, 2, 2, 3 * A_BLOCK, A_BLOCK).transpose(0, 2, 3, 1, 4)
    bias = bias.reshape(A_KV_HEADS, 2, 3 * A_BLOCK, 2 * A_BLOCK)
    sink = jnp.repeat(attn_sink.astype(F32) * LOG2E, A_BLOCK).reshape(A_KV_HEADS, 2, 2, A_BLOCK)
    sink = sink.transpose(0, 2, 1, 3).reshape(A_KV_HEADS, 2, 1, 2 * A_BLOCK)
    return bias, sink


RET_UNROLL = 32


def _ret_kernel(lg_ref, gc_ref, q_ref, k_ref, v_ref, g_ref, o_ref, st_ref, *, nc):
    h = pl.program_id(1)
    C = R_CHUNK
    lgf, lgb = lg_ref[0, h], lg_ref[1, h]
    gcf, gcb = gc_ref[0, h], gc_ref[1, h]
    diff = (lax.broadcasted_iota(jnp.int32, (C, C), 0)
            - lax.broadcasted_iota(jnp.int32, (C, C), 1)).astype(F32)
    decay = (jnp.where(diff >= 0, jnp.exp(jnp.maximum(diff, 0.0) * lgf), 0.0)
             + jnp.where(diff <= 0, jnp.exp(jnp.maximum(-diff, 0.0) * lgb), 0.0))
    idx = lax.broadcasted_iota(jnp.int32, (C, R_DK), 0).astype(F32)
    wq2 = jnp.concatenate([jnp.exp((idx + 1.0) * lgf), jnp.exp((C - idx) * lgb)], axis=1)
    wk2 = jnp.concatenate([jnp.exp((C - 1.0 - idx) * lgf), jnp.exp(idx * lgb)], axis=1)

    def rows(n):
        return pl.ds(pl.multiple_of(n * C, C), C)

    def kv_body(i, carry):
        for u in range(RET_UNROLL):
            n = i * RET_UNROLL + u
            k = k_ref[0, rows(n), :] * (R_DK ** -0.5)
            kw = (jnp.concatenate([k, k], axis=1) * wk2).astype(BF16)
            st_ref[n] = _dot_tn(kw, v_ref[0, rows(n), :].astype(BF16))
        return carry

    lax.fori_loop(0, nc // RET_UNROLL, kv_body, 0)

    def scan_body(t, carry):
        sf, sb = carry
        nb = nc - 1 - t
        kvf = st_ref[t, :R_DK, :]
        st_ref[t, :R_DK, :] = sf
        kvb = st_ref[nb, R_DK:, :]
        st_ref[nb, R_DK:, :] = sb
        return sf * gcf + kvf, sb * gcb + kvb

    zero = jnp.zeros((R_DK, R_DV), F32)
    lax.fori_loop(0, nc, scan_body, (zero, zero))

    def out_body(i, carry):
        ns = [i * RET_UNROLL + u for u in range(RET_UNROLL)]
        qs = [q_ref[0, rows(n), :] for n in ns]
        inner = [_dot_nt(q.astype(BF16), (k_ref[0, rows(n), :] * (R_DK ** -0.5)).astype(BF16))
                 for n, q in zip(ns, qs)]
        lhs = [jnp.concatenate([s * decay, jnp.concatenate([q, q], axis=1) * wq2], axis=1).astype(BF16)
               for s, q in zip(inner, qs)]
        outs = [_dot(a, jnp.concatenate([v_ref[0, rows(n), :].astype(BF16),
                                         st_ref[n].astype(BF16)], axis=0))
                for n, a in zip(ns, lhs)]
        for n, o in zip(ns, outs):
            o = o * lax.rsqrt(jnp.mean(o * o, axis=-1, keepdims=True) + EPS)
            g = g_ref[0, rows(n), :]
            o_ref[0, rows(n), :] = (g * _sigmoid(g) * o).astype(o_ref.dtype)
        return carry

    lax.fori_loop(0, nc // RET_UNROLL, out_body, 0)


def _retention(pr, lg, gchunk):
    B, L, _ = pr.shape
    nc = L // R_CHUNK
    assert nc % RET_UNROLL == 0
    part = lambda p: pl.BlockSpec((1, L, R_DK), lambda b, h: (b, 0, p * R_HEADS + h))
    smem = pl.BlockSpec(memory_space=pltpu.SMEM)
    return pl.pallas_call(
        functools.partial(_ret_kernel, nc=nc),
        grid=(B, R_HEADS),
        in_specs=[smem, smem, part(0), part(1), part(2), part(3)],
        out_specs=pl.BlockSpec((1, L, R_DV), lambda b, h: (b, 0, h)),
        out_shape=jax.ShapeDtypeStruct((B, L, R_W), BF16),
        scratch_shapes=[pltpu.VMEM((nc, 2 * R_DK, R_DV), F32)],
        compiler_params=_params(2), name="retention",
    )(lg, gchunk, pr, pr, pr, pr)


NA_HG = 4
NA_UNROLL = 16


def _na_kernel(q_ref, k_ref, v_ref, bias_ref, o_ref, *, n_rows):
    keys = NA_ROWS * GRID_W
    head = lax.broadcasted_iota(jnp.int32, (GRID_W, NA_HG * HEAD_DIM), 1) // HEAD_DIM

    def qrows(r):
        return pl.ds(pl.multiple_of(r * GRID_W, GRID_W), GRID_W)

    def krows(r):
        rs = jnp.clip(r - NA_ROWS // 2, 0, n_rows - NA_ROWS)
        return pl.ds(pl.multiple_of(rs * GRID_W, GRID_W), keys)

    def scores(r):
        q = q_ref[0, qrows(r), :].astype(F32)
        qbd = jnp.concatenate([jnp.where(head == h, q, 0.0) for h in range(NA_HG)], axis=0).astype(BF16)
        return _dot_nt(k_ref[0, krows(r), :], qbd)

    def softmax(r, s):
        rs = jnp.clip(r - NA_ROWS // 2, 0, n_rows - NA_ROWS)
        s = s + jnp.concatenate(
            [bias_ref[0, rs - r + (NA_ROWS - 1) + j] for j in range(NA_ROWS)], axis=0)
        p = jnp.exp2(s - jnp.max(s, axis=0, keepdims=True))
        return p.astype(BF16), 1.0 / jnp.sum(p, axis=0, keepdims=True)

    def body(i, carry):
        rs_ = [i * NA_UNROLL + u for u in range(NA_UNROLL)]
        all_s = [scores(r) for r in rs_]
        all_p = [softmax(r, s) for r, s in zip(rs_, all_s)]
        all_o = [_dot_tn(p, v_ref[0, krows(r), :]) for r, (p, _) in zip(rs_, all_p)]
        for r, o, (_, inv) in zip(rs_, all_o, all_p):
            scale = jnp.transpose(jnp.broadcast_to(inv, (2 * HEAD_DIM, NA_HG * GRID_W)))
            o = o * jnp.concatenate([scale, scale], axis=1)
            out = o[:GRID_W]
            for h in range(1, NA_HG):
                out = jnp.where(head == h, o[h * GRID_W:(h + 1) * GRID_W], out)
            o_ref[0, qrows(r), :] = out.astype(o_ref.dtype)
        return carry

    lax.fori_loop(0, n_rows // NA_UNROLL, body, 0)


def _nattn(pn, bias):
    B, L, _ = pn.shape
    groups = N_HEADS // NA_HG
    assert (L // GRID_W) % NA_UNROLL == 0
    width = NA_HG * HEAD_DIM
    part = lambda p: pl.BlockSpec((1, L, width), lambda b, g: (b, 0, p * groups + g))
    return pl.pallas_call(
        functools.partial(_na_kernel, n_rows=L // GRID_W),
        grid=(B, groups),
        in_specs=[part(0), part(1), part(2),
                  pl.BlockSpec((1,) + bias.shape[1:], lambda b, g: (g, 0, 0, 0))],
        out_specs=pl.BlockSpec((1, L, width), lambda b, g: (b, 0, g)),
        out_shape=jax.ShapeDtypeStruct((B, L, N_W), BF16),
        compiler_params=_params(2), name="neighbourhood_attn",
    )(pn, pn, pn, bias)


def _na_bias(rpb):
    c = np.arange(GRID_W)
    cs = np.clip(c - NA_COLS // 2, 0, GRID_W - NA_COLS)
    valid = (c[:, None] >= cs[None, :]) & (c[:, None] < cs[None, :] + NA_COLS)
    off = c[:, None] - c[None, :] + NA_COLS - 1
    onehot = valid[None] & (off[None] == np.arange(2 * NA_COLS - 1)[:, None, None])
    t = jnp.einsum('hrd,dkc->hrkc', rpb.astype(F32) * LOG2E, jnp.asarray(onehot, F32),
                   precision=lax.Precision.HIGHEST)
    t = jnp.where(jnp.asarray(valid)[None, None], t, NEG)
    t = t.reshape(N_HEADS // NA_HG, NA_HG, 2 * NA_ROWS - 1, GRID_W, GRID_W)
    return t.transpose(0, 2, 3, 1, 4).reshape(N_HEADS // NA_HG, 2 * NA_ROWS - 1, GRID_W, NA_HG * GRID_W)


def _merge_kernel(x_ref, g_ref, ya_ref, yb_ref, yc_ref, wg_ref, wb_ref, wo_ref, o_ref):
    half = x_ref.shape[1] // 2
    parts = (slice(0, half), slice(half, 2 * half))
    merged = []
    for r in parts:
        h = _rms(x_ref[0, r, :], g_ref[...]).astype(BF16)
        acc = None
        for i, y_ref in enumerate((ya_ref, yb_ref, yc_ref)):
            gate = _sigmoid(_dot(h, wg_ref[:, i * D_MODEL:(i + 1) * D_MODEL]))
            t = gate * _dot(y_ref[0, r, :], wb_ref[i])
            acc = t if acc is None else acc + t
        merged.append(acc.astype(BF16))
    for r, m in zip(parts, merged):
        o_ref[0, r, :] = x_ref[0, r, :] + _dot(m, wo_ref[...])


def _merge(x, g, ya, yb, yc, wg, wb, wo, tm):
    B, L, _ = x.shape
    row = lambda b, i: (b, i, 0)
    y_spec = pl.BlockSpec((1, tm, BRANCH_W), row)
    return pl.pallas_call(
        _merge_kernel,
        grid=(B, L // tm),
        in_specs=[pl.BlockSpec((1, tm, D_MODEL), row), _resident((1, D_MODEL)), y_spec, y_spec, y_spec,
                  _resident(wg.shape), _resident(wb.shape), _resident(wo.shape)],
        out_specs=pl.BlockSpec((1, tm, D_MODEL), row),
        out_shape=jax.ShapeDtypeStruct(x.shape, F32),
        compiler_params=_params(2), name="merge_out",
    )(x, g, ya, yb, yc, wg, wb, wo)


def _xattn_kernel(x_ref, g_ref, wq_ref, mem_ref, gm_ref, wkv_ref, wo_ref, o_ref, kv_ref):
    @pl.when(pl.program_id(1) == 0)
    def _():
        hm = _rms(mem_ref[0], gm_ref[...]).astype(BF16)
        for c in range(0, 2 * D_MODEL, 2 * MXU_N):
            kv_ref[:, c:c + 2 * MXU_N] = _dot(hm, wkv_ref[:, c:c + 2 * MXU_N]).astype(BF16)

    x = x_ref[0]
    q = _dot(_rms(x, g_ref[...]).astype(BF16), wq_ref[...]).astype(BF16)
    cols = [slice(h * X_HEAD_DIM, (h + 1) * X_HEAD_DIM) for h in range(X_HEADS)]
    scores = [_dot_nt(q[:, c], kv_ref[:, c]) * (X_HEAD_DIM ** -0.5) for c in cols]
    probs = [jnp.exp(s - jnp.max(s, axis=-1, keepdims=True)) for s in scores]
    dens = [jnp.sum(p, axis=-1, keepdims=True) for p in probs]
    outs = [_dot(p.astype(BF16), kv_ref[:, D_MODEL + c.start:D_MODEL + c.stop])
            for p, c in zip(probs, cols)]
    attn = jnp.concatenate([(o / d).astype(BF16) for o, d in zip(outs, dens)], axis=1)
    o_ref[0] = x + _dot(attn, wo_ref[...])


def _xattn(x, g, wq, mem, g_mem, wkv, wo, tm):
    B, L, _ = x.shape
    row = lambda b, i: (b, i, 0)
    return pl.pallas_call(
        _xattn_kernel,
        grid=(B, L // tm),
        in_specs=[pl.BlockSpec((1, tm, D_MODEL), row), _resident((1, D_MODEL)), _resident(wq.shape),
                  pl.BlockSpec((1, MEM_LEN, D_MODEL), lambda b, i: (b, 0, 0)), _resident((1, D_MODEL)),
                  _resident(wkv.shape), _resident(wo.shape)],
        out_specs=pl.BlockSpec((1, tm, D_MODEL), row),
        out_shape=jax.ShapeDtypeStruct(x.shape, F32),
        scratch_shapes=[pltpu.VMEM((MEM_LEN, 2 * D_MODEL), BF16)],
        compiler_params=_params(2), name="mem_xattn",
    )(x, g, wq, mem, g_mem, wkv, wo)


def _ffn_kernel(x_ref, g_ref, wgu_ref, wd_ref, gf_ref, o_ref, act_ref, *, final_norm):
    x = x_ref[0]
    h = _rms(x, g_ref[...]).astype(BF16)
    for c in range(0, D_FF, MXU_N):
        gate = _dot(h, wgu_ref[:, c:c + MXU_N])
        up = _dot(h, wgu_ref[:, D_FF + c:D_FF + c + MXU_N])
        act_ref[:, c:c + MXU_N] = (gate * _sigmoid(gate) * up).astype(BF16)
    y = x + _dot(act_ref[...], wd_ref[...])
    if final_norm:
        y = _rms(y, gf_ref[...])
    o_ref[0] = y


def _ffn(x, g, wgu, wd, g_final, final_norm, tm):
    B, L, _ = x.shape
    row = lambda b, i: (b, i, 0)
    return pl.pallas_call(
        functools.partial(_ffn_kernel, final_norm=final_norm),
        grid=(B, L // tm),
        in_specs=[pl.BlockSpec((1, tm, D_MODEL), row), _resident((1, D_MODEL)),
                  _resident(wgu.shape), _resident(wd.shape), _resident((1, D_MODEL))],
        out_specs=pl.BlockSpec((1, tm, D_MODEL), row),
        out_shape=jax.ShapeDtypeStruct(x.shape, F32),
        scratch_shapes=[pltpu.VMEM((tm, D_FF), BF16)],
        compiler_params=_params(2), name="swiglu",
    )(x, g, wgu, wd, g_final)


TM = 1024


def _trunk(x, mem, layers, g_final):
    for l, p in enumerate(layers):
        pa, pr, pn = _proj(x, p["g_mix"], p["w_in"], TM)
        ya = _wattn(pa, p["a_bias"], p["a_sink"])
        yb = _retention(pr, p["r_lg"], p["r_gchunk"])
        yc = _nattn(pn, p["n_bias"])
        x = _merge(x, p["g_mix"], ya, yb, yc, p["w_gate"], p["w_branch"], p["w_mix_out"], TM)
        x = _xattn(x, p["g_xattn"], p["w_xq"], mem, p["g_mem"], p["w_xkv"], p["w_xo"], TM)
        x = _ffn(x, p["g_ffn"], p["w_gate_up"], p["w_down"], g_final, l == DEPTH - 1, TM)
    return x


def kernel(x_prompt, x_sample, mem_prompt, mem_sample, g_mix, w_in, attn_sink, ret_decay, na_rpb,
           w_branch, w_mix_out, g_xattn, g_mem, w_xq, w_xkv, w_xo, g_ffn, w_gate_up, w_down, g_final):
    col_scale = np.ones((W_MIX,), np.float32)
    col_scale[A_Q:A_Q + A_KV] = SCORE_SCALE
    col_scale[W_A + W_R:W_A + W_R + N_W] = SCORE_SCALE
    col_scale = jnp.asarray(col_scale)
    layers = []
    for l in range(DEPTH):
        a_bias, a_sink = _wattn_tables(attn_sink[l])
        lg = jax.nn.log_sigmoid(ret_decay[l].astype(F32))
        layers.append(dict(
            g_mix=g_mix[l].reshape(1, D_MODEL),
            w_in=(w_in[l, :, :W_MIX] * col_scale).astype(BF16), w_gate=w_in[l, :, W_MIX:].astype(BF16),
            a_bias=a_bias, a_sink=a_sink,
            r_lg=lg, r_gchunk=jnp.exp(R_CHUNK * lg),
            n_bias=_na_bias(na_rpb[l]),
            w_branch=w_branch[l].astype(BF16), w_mix_out=w_mix_out[l].astype(BF16),
            g_xattn=g_xattn[l].reshape(1, D_MODEL), g_mem=g_mem[l].reshape(1, D_MODEL),
            w_xq=w_xq[l].astype(BF16), w_xkv=w_xkv[l].astype(BF16), w_xo=w_xo[l].astype(BF16),
            g_ffn=g_ffn[l].reshape(1, D_MODEL),
            w_gate_up=w_gate_up[l].astype(BF16), w_down=w_down[l].astype(BF16)))
    gf = g_final.reshape(1, D_MODEL)
    return (_trunk(x_prompt, mem_prompt, layers, gf), _trunk(x_sample, mem_sample, layers, gf))
```

```python
import functools

import numpy as np
import jax
import jax.numpy as jnp
from jax import lax
from jax.experimental import pallas as pl
from jax.experimental.pallas import tpu as pltpu

F32 = jnp.float32
BF16 = jnp.bfloat16

D_MODEL = 1024
DEPTH = 2
EPS = 1e-6
HEAD_DIM = 64
A_HEADS = 8
A_KV_HEADS = 2
A_GROUPS = A_HEADS // A_KV_HEADS
A_WINDOW = 128
A_BLOCK = 128
R_HEADS = 4
R_DK = 128
R_DV = 128
R_CHUNK = 128
N_HEADS = 8
GRID_W = 64
NA_ROWS = 8
NA_COLS = 16
MEM_LEN = 256
X_HEADS = 4
X_HEAD_DIM = D_MODEL // X_HEADS
D_FF = 2816
A_Q = A_HEADS * HEAD_DIM
A_KV = A_KV_HEADS * HEAD_DIM
R_W = R_HEADS * R_DK
N_W = N_HEADS * HEAD_DIM
BRANCH_W = 512
assert A_Q == R_W == N_W == BRANCH_W
W_A = A_Q + 2 * A_KV
W_R = 4 * R_W
W_N = 3 * N_W
W_MIX = W_A + W_R + W_N
NEG = -1e30
LOG2E = 1.4426950408889634
SCORE_SCALE = LOG2E * HEAD_DIM ** -0.5

MXU_N = 256
VMEM_LIMIT = 56 * 1024 * 1024


def _params(n_axes):
    return pltpu.CompilerParams(
        dimension_semantics=("arbitrary",) * n_axes, vmem_limit_bytes=VMEM_LIMIT)


def _resident(shape):
    return pl.BlockSpec(shape, lambda *_: (0,) * len(shape), pipeline_mode=pl.Buffered(1))


def _rms(x, g):
    return x * lax.rsqrt(jnp.mean(x * x, axis=-1, keepdims=True) + EPS) * g


def _sigmoid(x):
    return 1.0 / (1.0 + jnp.exp(-x))


def _dot(a, b):
    return jnp.dot(a, b, preferred_element_type=F32)


def _dot_nt(a, b):
    return lax.dot_general(a, b, (((1,), (1,)), ((), ())), preferred_element_type=F32)


def _dot_tn(a, b):
    return lax.dot_general(a, b, (((0,), (0,)), ((), ())), preferred_element_type=F32)


def _proj_kernel(x_ref, g_ref, w_ref, oa_ref, or_ref, on_ref):
    h = _rms(x_ref[0], g_ref[...]).astype(BF16)
    col = 0
    for o_ref, width in ((oa_ref, W_A), (or_ref, W_R), (on_ref, W_N)):
        for c in range(0, width, 2 * MXU_N):
            n = min(2 * MXU_N, width - c)
            o_ref[0, :, c:c + n] = _dot(h, w_ref[:, col + c:col + c + n]).astype(o_ref.dtype)
        col += width


def _proj(x, g, w, tm):
    B, L, _ = x.shape
    row = lambda b, i: (b, i, 0)
    return pl.pallas_call(
        _proj_kernel,
        grid=(B, L // tm),
        in_specs=[pl.BlockSpec((1, tm, D_MODEL), row), _resident((1, D_MODEL)),
                  _resident((D_MODEL, W_MIX))],
        out_specs=[pl.BlockSpec((1, tm, W_A), row), pl.BlockSpec((1, tm, W_R), row),
                   pl.BlockSpec((1, tm, W_N), row)],
        out_shape=[jax.ShapeDtypeStruct((B, L, W_A), BF16), jax.ShapeDtypeStruct((B, L, W_R), F32),
                   jax.ShapeDtypeStruct((B, L, W_N), BF16)],
        compiler_params=_params(2), name="proj_in",
    )(x, g, w)


WA_QB = 16
WA_STAGE = 4


def _wattn_kernel(q_ref, kp_ref, kc_ref, kn_ref, vp_ref, vc_ref, vn_ref, bias_ref, sink_ref, ones_ref,
                  o_ref, *, seq_len):
    n = pl.program_id(1)
    win = 3 * A_BLOCK
    k = jnp.concatenate([kp_ref[0], kc_ref[0], kn_ref[0]], axis=0).astype(F32)
    v = jnp.concatenate([vp_ref[0], vc_ref[0], vn_ref[0]], axis=0).astype(F32)
    lo = lax.broadcasted_iota(jnp.int32, k.shape, 1) < HEAD_DIM
    swap = lambda x: jnp.concatenate([x[:, HEAD_DIM:], x[:, :HEAD_DIM]], axis=1)
    ks, vs = swap(k), swap(v)
    k_lo = [jnp.where(lo, k, 0.0).astype(BF16), jnp.where(lo, ks, 0.0).astype(BF16)]
    k_hi = [jnp.where(lo, 0.0, ks).astype(BF16), jnp.where(lo, 0.0, k).astype(BF16)]
    v_lo = [jnp.where(lo, v, 0.0).astype(BF16), jnp.where(lo, vs, 0.0).astype(BF16)]
    v_hi = [jnp.where(lo, 0.0, vs).astype(BF16), jnp.where(lo, 0.0, v).astype(BF16)]
    j = lax.broadcasted_iota(jnp.int32, (win, 1), 0)
    even_rows = lax.broadcasted_iota(jnp.int32, (A_KV, 2 * A_BLOCK), 0) < HEAD_DIM
    units = [(i, h) for i in range(WA_QB) for h in range(A_KV_HEADS)]
    wrows = lambda i: slice(i * A_BLOCK, i * A_BLOCK + win)

    def scores(i, h):
        rows = slice(i * A_BLOCK, (i + 1) * A_BLOCK)
        c0 = h * A_GROUPS * HEAD_DIM
        qh = jnp.concatenate([q_ref[0, rows, c0:c0 + 2 * HEAD_DIM],
                              q_ref[0, rows, c0 + 2 * HEAD_DIM:c0 + 4 * HEAD_DIM]], axis=0)
        return [_dot_nt(kw[wrows(i)], qh) for kw in (k_lo[h], k_hi[h])]

    def softmax(i, h, par, s):
        s = s + bias_ref[h, par]
        if i in (0, WA_QB - 1):
            kpos = (n * WA_QB + i - 1) * A_BLOCK + j
            s = jnp.where((kpos >= 0) & (kpos < seq_len), s, NEG)
        sk = sink_ref[h, par]
        m = jnp.maximum(jnp.max(s, axis=0, keepdims=True), sk)
        return jnp.exp2(s - m).astype(BF16), jnp.exp2(sk - m)

    for g0 in range(0, len(units), WA_STAGE):
        grp = units[g0:g0 + WA_STAGE]
        all_s = [scores(i, h) for i, h in grp]
        all_p, all_e = [], []
        for (i, h), ss in zip(grp, all_s):
            (pe, ee), (po, eo) = [softmax(i, h, par, s) for par, s in enumerate(ss)]
            all_p.append(jnp.concatenate([pe, po], axis=0))
            all_e.append(jnp.where(even_rows, ee, eo))
        all_o = [_dot_tn(p, jnp.concatenate(
                     [jnp.concatenate([v_lo[h][wrows(i)], v_hi[h][wrows(i)]], axis=0), ones_ref[...]], axis=1))
                 for (i, h), p in zip(grp, all_p)]
        for (i, h), res, e in zip(grp, all_o, all_e):
            res = res[:, :A_KV] * (1.0 / (res[:, A_KV:] + jnp.transpose(e)))
            c0 = h * A_GROUPS * HEAD_DIM
            o_ref[0, i * A_BLOCK:(i + 1) * A_BLOCK, c0:c0 + 4 * HEAD_DIM] = jnp.concatenate(
                [res[:A_BLOCK], res[A_BLOCK:]], axis=1).astype(o_ref.dtype)


def _wattn(pa, bias, sink):
    B, L, _ = pa.shape
    ones = np.zeros((6 * A_BLOCK, A_KV), np.float32)
    ones[:3 * A_BLOCK, :HEAD_DIM] = 1.0
    ones[3 * A_BLOCK:, HEAD_DIM:] = 1.0
    ones = jnp.asarray(ones, BF16)
    nb = L // A_BLOCK
    assert nb % WA_QB == 0
    kcol, vcol = A_Q // A_KV, A_Q // A_KV + 1
    prev = lambda c: pl.BlockSpec((1, A_BLOCK, A_KV), lambda b, n: (b, jnp.maximum(n * WA_QB - 1, 0), c))
    cur = lambda c: pl.BlockSpec((1, WA_QB * A_BLOCK, A_KV), lambda b, n: (b, n, c))
    nxt = lambda c: pl.BlockSpec((1, A_BLOCK, A_KV),
                                 lambda b, n: (b, jnp.minimum((n + 1) * WA_QB, nb - 1), c))
    return pl.pallas_call(
        functools.partial(_wattn_kernel, seq_len=L),
        grid=(B, nb // WA_QB),
        in_specs=[pl.BlockSpec((1, WA_QB * A_BLOCK, A_Q), lambda b, n: (b, n, 0)),
                  prev(kcol), cur(kcol), nxt(kcol), prev(vcol), cur(vcol), nxt(vcol),
                  _resident(bias.shape), _resident(sink.shape), _resident(ones.shape)],
        out_specs=pl.BlockSpec((1, WA_QB * A_BLOCK, A_Q), lambda b, n: (b, n, 0)),
        out_shape=jax.ShapeDtypeStruct((B, L, A_Q), BF16),
        compiler_params=_params(2), name="window_gqa",
    )(pa, pa, pa, pa, pa, pa, pa, bias, sink, ones)


def _wattn_tables(attn_sink):
    i = np.arange(A_BLOCK)[None, :]
    j = np.arange(3 * A_BLOCK)[:, None]
    dist = np.abs(A_BLOCK + i - j)
    slopes = jnp.exp2(-8.0 * jnp.arange(1, A_HEADS + 1, dtype=F32) / A_HEADS)
    bias = -slopes[:, None, None] * jnp.asarray(dist, F32)[None]
    bias = jnp.where(jnp.asarray(dist <= A_WINDOW)[None], bias * LOG2E, NEG)
    bias = bias.reshape(A_KV_HEADS, 2, 2, 3 * A_BLOCK, A_BLOCK).transpose(0, 2, 3, 1, 4)
    bias = bias.reshape(A_KV_HEADS, 2, 3 * A_BLOCK, 2 * A_BLOCK)
    sink = jnp.repeat(attn_sink.astype(F32) * LOG2E, A_BLOCK).reshape(A_KV_HEADS, 2, 2, A_BLOCK)
    sink = sink.transpose(0, 2, 1, 3).reshape(A_KV_HEADS, 2, 1, 2 * A_BLOCK)
    return bias, sink


RET_UNROLL = 32


def _ret_kernel(lg_ref, gc_ref, q_ref, k_ref, v_ref, g_ref, o_ref, st_ref, *, nc):
    h = pl.program_id(1)
    C = R_CHUNK
    lgf, lgb = lg_ref[0, h], lg_ref[1, h]
    gcf, gcb = gc_ref[0, h], gc_ref[1, h]
    diff = (lax.broadcasted_iota(jnp.int32, (C, C), 0)
            - lax.broadcasted_iota(jnp.int32, (C, C), 1)).astype(F32)
    decay = (jnp.where(diff >= 0, jnp.exp(jnp.maximum(diff, 0.0) * lgf), 0.0)
             + jnp.where(diff <= 0, jnp.exp(jnp.maximum(-diff, 0.0) * lgb), 0.0))
    idx = lax.broadcasted_iota(jnp.int32, (C, R_DK), 0).astype(F32)
    wq2 = jnp.concatenate([jnp.exp((idx + 1.0) * lgf), jnp.exp((C - idx) * lgb)], axis=1)
    wk2 = jnp.concatenate([jnp.exp((C - 1.0 - idx) * lgf), jnp.exp(idx * lgb)], axis=1)

    def rows(n):
        return pl.ds(pl.multiple_of(n * C, C), C)

    def kv_body(i, carry):
        for u in range(RET_UNROLL):
            n = i * RET_UNROLL + u
            k = k_ref[0, rows(n), :] * (R_DK ** -0.5)
            kw = (jnp.concatenate([k, k], axis=1) * wk2).astype(BF16)
            st_ref[n] = _dot_tn(kw, v_ref[0, rows(n), :].astype(BF16))
        return carry

    lax.fori_loop(0, nc // RET_UNROLL, kv_body, 0)

    def scan_body(t, carry):
        sf, sb = carry
        nb = nc - 1 - t
        kvf = st_ref[t, :R_DK, :]
        st_ref[t, :R_DK, :] = sf
        kvb = st_ref[nb, R_DK:, :]
        st_ref[nb, R_DK:, :] = sb
        return sf * gcf + kvf, sb * gcb + kvb

    zero = jnp.zeros((R_DK, R_DV), F32)
    lax.fori_loop(0, nc, scan_body, (zero, zero))

    def out_body(i, carry):
        ns = [i * RET_UNROLL + u for u in range(RET_UNROLL)]
        qs = [q_ref[0, rows(n), :] for n in ns]
        inner = [_dot_nt(q.astype(BF16), (k_ref[0, rows(n), :] * (R_DK ** -0.5)).astype(BF16))
                 for n, q in zip(ns, qs)]
        lhs = [jnp.concatenate([s * decay, jnp.concatenate([q, q], axis=1) * wq2], axis=1).astype(BF16)
               for s, q in zip(inner, qs)]
        outs = [_dot(a, jnp.concatenate([v_ref[0, rows(n), :].astype(BF16),
                                         st_ref[n].astype(BF16)], axis=0))
                for n, a in zip(ns, lhs)]
        for n, o in zip(ns, outs):
            o = o * lax.rsqrt(jnp.mean(o * o, axis=-1, keepdims=True) + EPS)
            g = g_ref[0, rows(n), :]
            o_ref[0, rows(n), :] = (g * _sigmoid(g) * o).astype(o_ref.dtype)
        return carry

    lax.fori_loop(0, nc // RET_UNROLL, out_body, 0)


def _retention(pr, lg, gchunk):
    B, L, _ = pr.shape
    nc = L // R_CHUNK
    assert nc % RET_UNROLL == 0
    part = lambda p: pl.BlockSpec((1, L, R_DK), lambda b, h: (b, 0, p * R_HEADS + h))
    smem = pl.BlockSpec(memory_space=pltpu.SMEM)
    return pl.pallas_call(
        functools.partial(_ret_kernel, nc=nc),
        grid=(B, R_HEADS),
        in_specs=[smem, smem, part(0), part(1), part(2), part(3)],
        out_specs=pl.BlockSpec((1, L, R_DV), lambda b, h: (b, 0, h)),
        out_shape=jax.ShapeDtypeStruct((B, L, R_W), BF16),
        scratch_shapes=[pltpu.VMEM((nc, 2 * R_DK, R_DV), F32)],
        compiler_params=_params(2), name="retention",
    )(lg, gchunk, pr, pr, pr, pr)


NA_HG = 4
NA_UNROLL = 16


def _na_kernel(q_ref, k_ref, v_ref, bias_ref, o_ref, *, n_rows):
    keys = NA_ROWS * GRID_W
    head = lax.broadcasted_iota(jnp.int32, (GRID_W, NA_HG * HEAD_DIM), 1) // HEAD_DIM

    def qrows(r):
        return pl.ds(pl.multiple_of(r * GRID_W, GRID_W), GRID_W)

    def krows(r):
        rs = jnp.clip(r - NA_ROWS // 2, 0, n_rows - NA_ROWS)
        return pl.ds(pl.multiple_of(rs * GRID_W, GRID_W), keys)

    def scores(r):
        q = q_ref[0, qrows(r), :].astype(F32)
        qbd = jnp.concatenate([jnp.where(head == h, q, 0.0) for h in range(NA_HG)], axis=0).astype(BF16)
        return _dot_nt(k_ref[0, krows(r), :], qbd)

    def softmax(r, s):
        rs = jnp.clip(r - NA_ROWS // 2, 0, n_rows - NA_ROWS)
        s = s + jnp.concatenate(
            [bias_ref[0, rs - r + (NA_ROWS - 1) + j] for j in range(NA_ROWS)], axis=0)
        p = jnp.exp2(s - jnp.max(s, axis=0, keepdims=True))
        return p.astype(BF16), 1.0 / jnp.sum(p, axis=0, keepdims=True)

    def body(i, carry):
        rs_ = [i * NA_UNROLL + u for u in range(NA_UNROLL)]
        all_s = [scores(r) for r in rs_]
        all_p = [softmax(r, s) for r, s in zip(rs_, all_s)]
        all_o = [_dot_tn(p, v_ref[0, krows(r), :]) for r, (p, _) in zip(rs_, all_p)]
        for r, o, (_, inv) in zip(rs_, all_o, all_p):
            scale = jnp.transpose(jnp.broadcast_to(inv, (2 * HEAD_DIM, NA_HG * GRID_W)))
            o = o * jnp.concatenate([scale, scale], axis=1)
            out = o[:GRID_W]
            for h in range(1, NA_HG):
                out = jnp.where(head == h, o[h * GRID_W:(h + 1) * GRID_W], out)
            o_ref[0, qrows(r), :] = out.astype(o_ref.dtype)
        return carry

    lax.fori_loop(0, n_rows // NA_UNROLL, body, 0)


def _nattn(pn, bias):
    B, L, _ = pn.shape
    groups = N_HEADS // NA_HG
    assert (L // GRID_W) % NA_UNROLL == 0
    width = NA_HG * HEAD_DIM
    part = lambda p: pl.BlockSpec((1, L, width), lambda b, g: (b, 0, p * groups + g))
    return pl.pallas_call(
        functools.partial(_na_kernel, n_rows=L // GRID_W),
        grid=(B, groups),
        in_specs=[part(0), part(1), part(2),
                  pl.BlockSpec((1,) + bias.shape[1:], lambda b, g: (g, 0, 0, 0))],
        out_specs=pl.BlockSpec((1, L, width), lambda b, g: (b, 0, g)),
        out_shape=jax.ShapeDtypeStruct((B, L, N_W), BF16),
        compiler_params=_params(2), name="neighbourhood_attn",
    )(pn, pn, pn, bias)


def _na_bias(rpb):
    c = np.arange(GRID_W)
    cs = np.clip(c - NA_COLS // 2, 0, GRID_W - NA_COLS)
    valid = (c[:, None] >= cs[None, :]) & (c[:, None] < cs[None, :] + NA_COLS)
    off = c[:, None] - c[None, :] + NA_COLS - 1
    onehot = valid[None] & (off[None] == np.arange(2 * NA_COLS - 1)[:, None, None])
    t = jnp.einsum('hrd,dkc->hrkc', rpb.astype(F32) * LOG2E, jnp.asarray(onehot, F32),
                   precision=lax.Precision.HIGHEST)
    t = jnp.where(jnp.asarray(valid)[None, None], t, NEG)
    t = t.reshape(N_HEADS // NA_HG, NA_HG, 2 * NA_ROWS - 1, GRID_W, GRID_W)
    return t.transpose(0, 2, 3, 1, 4).reshape(N_HEADS // NA_HG, 2 * NA_ROWS - 1, GRID_W, NA_HG * GRID_W)


def _merge_kernel(x_ref, g_ref, ya_ref, yb_ref, yc_ref, wg_ref, wb_ref, wo_ref, o_ref):
    half = x_ref.shape[1] // 2
    parts = (slice(0, half), slice(half, 2 * half))
    merged = []
    for r in parts:
        h = _rms(x_ref[0, r, :], g_ref[...]).astype(BF16)
        acc = None
        for i, y_ref in enumerate((ya_ref, yb_ref, yc_ref)):
            gate = _sigmoid(_dot(h, wg_ref[:, i * D_MODEL:(i + 1) * D_MODEL]))
            t = gate * _dot(y_ref[0, r, :], wb_ref[i])
            acc = t if acc is None else acc + t
        merged.append(acc.astype(BF16))
    for r, m in zip(parts, merged):
        o_ref[0, r, :] = x_ref[0, r, :] + _dot(m, wo_ref[...])


def _merge(x, g, ya, yb, yc, wg, wb, wo, tm):
    B, L, _ = x.shape
    row = lambda b, i: (b, i, 0)
    y_spec = pl.BlockSpec((1, tm, BRANCH_W), row)
    return pl.pallas_call(
        _merge_kernel,
        grid=(B, L // tm),
        in_specs=[pl.BlockSpec((1, tm, D_MODEL), row), _resident((1, D_MODEL)), y_spec, y_spec, y_spec,
                  _resident(wg.shape), _resident(wb.shape), _resident(wo.shape)],
        out_specs=pl.BlockSpec((1, tm, D_MODEL), row),
        out_shape=jax.ShapeDtypeStruct(x.shape, F32),
        compiler_params=_params(2), name="merge_out",
    )(x, g, ya, yb, yc, wg, wb, wo)


def _xattn_kernel(x_ref, g_ref, wq_ref, mem_ref, gm_ref, wkv_ref, wo_ref, o_ref, kv_ref):
    @pl.when(pl.program_id(1) == 0)
    def _():
        hm = _rms(mem_ref[0], gm_ref[...]).astype(BF16)
        for c in range(0, 2 * D_MODEL, 2 * MXU_N):
            kv_ref[:, c:c + 2 * MXU_N] = _dot(hm, wkv_ref[:, c:c + 2 * MXU_N]).astype(BF16)

    x = x_ref[0]
    q = _dot(_rms(x, g_ref[...]).astype(BF16), wq_ref[...]).astype(BF16)
    cols = [slice(h * X_HEAD_DIM, (h + 1) * X_HEAD_DIM) for h in range(X_HEADS)]
    scores = [_dot_nt(q[:, c], kv_ref[:, c]) * (X_HEAD_DIM ** -0.5) for c in cols]
    probs = [jnp.exp(s - jnp.max(s, axis=-1, keepdims=True)) for s in scores]
    dens = [jnp.sum(p, axis=-1, keepdims=True) for p in probs]
    outs = [_dot(p.astype(BF16), kv_ref[:, D_MODEL + c.start:D_MODEL + c.stop])
            for p, c in zip(probs, cols)]
    attn = jnp.concatenate([(o / d).astype(BF16) for o, d in zip(outs, dens)], axis=1)
    o_ref[0] = x + _dot(attn, wo_ref[...])


def _xattn(x, g, wq, mem, g_mem, wkv, wo, tm):
    B, L, _ = x.shape
    row = lambda b, i: (b, i, 0)
    return pl.pallas_call(
        _xattn_kernel,
        grid=(B, L // tm),
        in_specs=[pl.BlockSpec((1, tm, D_MODEL), row), _resident((1, D_MODEL)), _resident(wq.shape),
                  pl.BlockSpec((1, MEM_LEN, D_MODEL), lambda b, i: (b, 0, 0)), _resident((1, D_MODEL)),
                  _resident(wkv.shape), _resident(wo.shape)],
        out_specs=pl.BlockSpec((1, tm, D_MODEL), row),
        out_shape=jax.ShapeDtypeStruct(x.shape, F32),
        scratch_shapes=[pltpu.VMEM((MEM_LEN, 2 * D_MODEL), BF16)],
        compiler_params=_params(2), name="mem_xattn",
    )(x, g, wq, mem, g_mem, wkv, wo)


def _ffn_kernel(x_ref, g_ref, wgu_ref, wd_ref, gf_ref, o_ref, act_ref, *, final_norm):
    x = x_ref[0]
    h = _rms(x, g_ref[...]).astype(BF16)
    for c in range(0, D_FF, MXU_N):
        gate = _dot(h, wgu_ref[:, c:c + MXU_N])
        up = _dot(h, wgu_ref[:, D_FF + c:D_FF + c + MXU_N])
        act_ref[:, c:c + MXU_N] = (gate * _sigmoid(gate) * up).astype(BF16)
    y = x + _dot(act_ref[...], wd_ref[...])
    if final_norm:
        y = _rms(y, gf_ref[...])
    o_ref[0] = y


def _ffn(x, g, wgu, wd, g_final, final_norm, tm):
    B, L, _ = x.shape
    row = lambda b, i: (b, i, 0)
    return pl.pallas_call(
        functools.partial(_ffn_kernel, final_norm=final_norm),
        grid=(B, L // tm),
        in_specs=[pl.BlockSpec((1, tm, D_MODEL), row), _resident((1, D_MODEL)),
                  _resident(wgu.shape), _resident(wd.shape), _resident((1, D_MODEL))],
        out_specs=pl.BlockSpec((1, tm, D_MODEL), row),
        out_shape=jax.ShapeDtypeStruct(x.shape, F32),
        scratch_shapes=[pltpu.VMEM((tm, D_FF), BF16)],
        compiler_params=_params(2), name="swiglu",
    )(x, g, wgu, wd, g_final)


CAST_STEPS = 16


def _cast_kernel(scale_ref, win_ref, *refs):
    n_others = (len(refs) - 2 * DEPTH) // (1 + DEPTH)
    others, outs = refs[:n_others], refs[n_others:]
    k = 0
    for l in range(DEPTH):
        outs[k][...] = (win_ref[l, :, :W_MIX] * scale_ref[...]).astype(BF16)
        outs[k + 1][...] = win_ref[l, :, W_MIX:].astype(BF16)
        k += 2
    for w_ref in others:
        for l in range(DEPTH):
            outs[k][...] = w_ref[l].astype(BF16)
            k += 1


def _cast_weights(col_scale, w_in, others):
    spec_in = lambda w: pl.BlockSpec((DEPTH, w.shape[1] // CAST_STEPS, w.shape[2]), lambda s: (0, s, 0))
    outs = []
    for w, cols in [(w_in, W_MIX), (w_in, w_in.shape[2] - W_MIX)] * DEPTH + [
            (w, w.shape[2]) for w in others for _ in range(DEPTH)]:
        assert w.shape[1] % (16 * CAST_STEPS) == 0
        outs.append((jax.ShapeDtypeStruct((w.shape[1], cols), BF16),
                     pl.BlockSpec((w.shape[1] // CAST_STEPS, cols), lambda s: (s, 0))))
    res = pl.pallas_call(
        _cast_kernel,
        grid=(CAST_STEPS,),
        in_specs=[_resident(col_scale.shape), spec_in(w_in)] + [spec_in(w) for w in others],
        out_specs=[o[1] for o in outs],
        out_shape=[o[0] for o in outs],
        compiler_params=_params(1), name="cast_weights",
    )(col_scale, w_in, *others)
    w_mix = [res[2 * l] for l in range(DEPTH)]
    w_gate = [res[2 * l + 1] for l in range(DEPTH)]
    rest = [[res[2 * DEPTH + i * DEPTH + l] for l in range(DEPTH)] for i in range(len(others))]
    return w_mix, w_gate, rest


TM = 1024


def _trunk(x, mem, layers, g_final):
    for l, p in enumerate(layers):
        pa, pr, pn = _proj(x, p["g_mix"], p["w_in"], TM)
        ya = _wattn(pa, p["a_bias"], p["a_sink"])
        yb = _retention(pr, p["r_lg"], p["r_gchunk"])
        yc = _nattn(pn, p["n_bias"])
        x = _merge(x, p["g_mix"], ya, yb, yc, p["w_gate"], p["w_branch"], p["w_mix_out"], TM)
        x = _xattn(x, p["g_xattn"], p["w_xq"], mem, p["g_mem"], p["w_xkv"], p["w_xo"], TM)
        x = _ffn(x, p["g_ffn"], p["w_gate_up"], p["w_down"], g_final, l == DEPTH - 1, TM)
    return x


def kernel(x_prompt, x_sample, mem_prompt, mem_sample, g_mix, w_in, attn_sink, ret_decay, na_rpb,
           w_branch, w_mix_out, g_xattn, g_mem, w_xq, w_xkv, w_xo, g_ffn, w_gate_up, w_down, g_final):
    col_scale = np.ones((W_MIX,), np.float32)
    col_scale[A_Q:A_Q + A_KV] = SCORE_SCALE
    col_scale[W_A + W_R:W_A + W_R + N_W] = SCORE_SCALE
    w_mix, w_gate, (wb, wmo, wxq, wxkv, wxo, wgu, wd) = _cast_weights(
        jnp.asarray(col_scale).reshape(1, W_MIX), w_in,
        [w_branch.reshape(DEPTH, 3 * BRANCH_W, D_MODEL), w_mix_out, w_xq, w_xkv, w_xo, w_gate_up, w_down])
    layers = []
    for l in range(DEPTH):
        a_bias, a_sink = _wattn_tables(attn_sink[l])
        lg = jax.nn.log_sigmoid(ret_decay[l].astype(F32))
        layers.append(dict(
            g_mix=g_mix[l].reshape(1, D_MODEL),
            w_in=w_mix[l], w_gate=w_gate[l],
            a_bias=a_bias, a_sink=a_sink,
            r_lg=lg, r_gchunk=jnp.exp(R_CHUNK * lg),
            n_bias=_na_bias(na_rpb[l]),
            w_branch=wb[l].reshape(3, BRANCH_W, D_MODEL), w_mix_out=wmo[l],
            g_xattn=g_xattn[l].reshape(1, D_MODEL), g_mem=g_mem[l].reshape(1, D_MODEL),
            w_xq=wxq[l], w_xkv=wxkv[l], w_xo=wxo[l],
            g_ffn=g_ffn[l].reshape(1, D_MODEL),
            w_gate_up=wgu[l], w_down=wd[l]))
    gf = g_final.reshape(1, D_MODEL)
    return (_trunk(x_prompt, mem_prompt, layers, gf), _trunk(x_sample, mem_sample, layers, gf))
```

```python
import functools

import numpy as np
import jax
import jax.numpy as jnp
from jax import lax
from jax.experimental import pallas as pl
from jax.experimental.pallas import tpu as pltpu

F32 = jnp.float32
BF16 = jnp.bfloat16

D_MODEL = 1024
DEPTH = 2
EPS = 1e-6
HEAD_DIM = 64
A_HEADS = 8
A_KV_HEADS = 2
A_GROUPS = A_HEADS // A_KV_HEADS
A_WINDOW = 128
A_BLOCK = 128
R_HEADS = 4
R_DK = 128
R_DV = 128
R_CHUNK = 128
N_HEADS = 8
GRID_W = 64
NA_ROWS = 8
NA_COLS = 16
MEM_LEN = 256
X_HEADS = 4
X_HEAD_DIM = D_MODEL // X_HEADS
D_FF = 2816
A_Q = A_HEADS * HEAD_DIM
A_KV = A_KV_HEADS * HEAD_DIM
R_W = R_HEADS * R_DK
N_W = N_HEADS * HEAD_DIM
BRANCH_W = 512
assert A_Q == R_W == N_W == BRANCH_W
W_A = A_Q + 2 * A_KV
W_R = 4 * R_W
W_N = 3 * N_W
W_MIX = W_A + W_R + W_N
NEG = -1e30
LOG2E = 1.4426950408889634
SCORE_SCALE = LOG2E * HEAD_DIM ** -0.5

MXU_N = 256
VMEM_LIMIT = 56 * 1024 * 1024


def _params(n_axes):
    return pltpu.CompilerParams(
        dimension_semantics=("arbitrary",) * n_axes, vmem_limit_bytes=VMEM_LIMIT)


def _resident(shape):
    return pl.BlockSpec(shape, lambda *_: (0,) * len(shape), pipeline_mode=pl.Buffered(1))


def _rms(x, g):
    return x * lax.rsqrt(jnp.mean(x * x, axis=-1, keepdims=True) + EPS) * g


def _sigmoid(x):
    return 1.0 / (1.0 + jnp.exp(-x))


def _dot(a, b):
    return jnp.dot(a, b, preferred_element_type=F32)


def _dot_nt(a, b):
    return lax.dot_general(a, b, (((1,), (1,)), ((), ())), preferred_element_type=F32)


def _dot_tn(a, b):
    return lax.dot_general(a, b, (((0,), (0,)), ((), ())), preferred_element_type=F32)


def _proj_kernel(x_ref, g_ref, w_ref, oa_ref, or_ref, on_ref):
    half = x_ref.shape[1] // 2
    for r in (slice(0, half), slice(half, 2 * half)):
        h = _rms(x_ref[0, r, :], g_ref[...]).astype(BF16)
        col = 0
        for o_ref, width in ((oa_ref, W_A), (or_ref, W_R), (on_ref, W_N)):
            for c in range(0, width, 2 * MXU_N):
                n = min(2 * MXU_N, width - c)
                o_ref[0, r, c:c + n] = _dot(h, w_ref[:, col + c:col + c + n]).astype(o_ref.dtype)
            col += width


def _proj(x, g, w, tm):
    B, L, _ = x.shape
    row = lambda b, i: (b, i, 0)
    return pl.pallas_call(
        _proj_kernel,
        grid=(B, L // tm),
        in_specs=[pl.BlockSpec((1, tm, D_MODEL), row), _resident((1, D_MODEL)),
                  _resident((D_MODEL, W_MIX))],
        out_specs=[pl.BlockSpec((1, tm, W_A), row), pl.BlockSpec((1, tm, W_R), row),
                   pl.BlockSpec((1, tm, W_N), row)],
        out_shape=[jax.ShapeDtypeStruct((B, L, W_A), BF16), jax.ShapeDtypeStruct((B, L, W_R), F32),
                   jax.ShapeDtypeStruct((B, L, W_N), BF16)],
        compiler_params=_params(2), name="proj_in",
    )(x, g, w)


WA_QB = 16
WA_STAGE = 4


def _wattn_kernel(q_ref, kp_ref, kc_ref, kn_ref, vp_ref, vc_ref, vn_ref, bias_ref, sink_ref, ones_ref,
                  o_ref, *, seq_len):
    n = pl.program_id(1)
    win = 3 * A_BLOCK
    k = jnp.concatenate([kp_ref[0], kc_ref[0], kn_ref[0]], axis=0).astype(F32)
    v = jnp.concatenate([vp_ref[0], vc_ref[0], vn_ref[0]], axis=0).astype(F32)
    lo = lax.broadcasted_iota(jnp.int32, k.shape, 1) < HEAD_DIM
    swap = lambda x: jnp.concatenate([x[:, HEAD_DIM:], x[:, :HEAD_DIM]], axis=1)
    ks, vs = swap(k), swap(v)
    k_lo = [jnp.where(lo, k, 0.0).astype(BF16), jnp.where(lo, ks, 0.0).astype(BF16)]
    k_hi = [jnp.where(lo, 0.0, ks).astype(BF16), jnp.where(lo, 0.0, k).astype(BF16)]
    v_lo = [jnp.where(lo, v, 0.0).astype(BF16), jnp.where(lo, vs, 0.0).astype(BF16)]
    v_hi = [jnp.where(lo, 0.0, vs).astype(BF16), jnp.where(lo, 0.0, v).astype(BF16)]
    j = lax.broadcasted_iota(jnp.int32, (win, 1), 0)
    even_rows = lax.broadcasted_iota(jnp.int32, (A_KV, 2 * A_BLOCK), 0) < HEAD_DIM
    units = [(i, h) for i in range(WA_QB) for h in range(A_KV_HEADS)]
    wrows = lambda i: slice(i * A_BLOCK, i * A_BLOCK + win)

    def scores(i, h):
        rows = slice(i * A_BLOCK, (i + 1) * A_BLOCK)
        c0 = h * A_GROUPS * HEAD_DIM
        qh = jnp.concatenate([q_ref[0, rows, c0:c0 + 2 * HEAD_DIM],
                              q_ref[0, rows, c0 + 2 * HEAD_DIM:c0 + 4 * HEAD_DIM]], axis=0)
        return [_dot_nt(kw[wrows(i)], qh) for kw in (k_lo[h], k_hi[h])]

    def softmax(i, h, par, s):
        s = s + bias_ref[h, par]
        if i in (0, WA_QB - 1):
            kpos = (n * WA_QB + i - 1) * A_BLOCK + j
            s = jnp.where((kpos >= 0) & (kpos < seq_len), s, NEG)
        sk = sink_ref[h, par]
        m = jnp.maximum(jnp.max(s, axis=0, keepdims=True), sk)
        return jnp.exp2(s - m).astype(BF16), jnp.exp2(sk - m)

    for g0 in range(0, len(units), WA_STAGE):
        grp = units[g0:g0 + WA_STAGE]
        all_s = [scores(i, h) for i, h in grp]
        all_p, all_e = [], []
        for (i, h), ss in zip(grp, all_s):
            (pe, ee), (po, eo) = [softmax(i, h, par, s) for par, s in enumerate(ss)]
            all_p.append(jnp.concatenate([pe, po], axis=0))
            all_e.append(jnp.where(even_rows, ee, eo))
        all_o = [_dot_tn(p, jnp.concatenate(
                     [jnp.concatenate([v_lo[h][wrows(i)], v_hi[h][wrows(i)]], axis=0), ones_ref[...]], axis=1))
                 for (i, h), p in zip(grp, all_p)]
        for (i, h), res, e in zip(grp, all_o, all_e):
            res = res[:, :A_KV] * (1.0 / (res[:, A_KV:] + jnp.transpose(e)))
            c0 = h * A_GROUPS * HEAD_DIM
            o_ref[0, i * A_BLOCK:(i + 1) * A_BLOCK, c0:c0 + 4 * HEAD_DIM] = jnp.concatenate(
                [res[:A_BLOCK], res[A_BLOCK:]], axis=1).astype(o_ref.dtype)


def _wattn(pa, bias, sink):
    B, L, _ = pa.shape
    ones = np.zeros((6 * A_BLOCK, A_KV), np.float32)
    ones[:3 * A_BLOCK, :HEAD_DIM] = 1.0
    ones[3 * A_BLOCK:, HEAD_DIM:] = 1.0
    ones = jnp.asarray(ones, BF16)
    nb = L // A_BLOCK
    assert nb % WA_QB == 0
    kcol, vcol = A_Q // A_KV, A_Q // A_KV + 1
    prev = lambda c: pl.BlockSpec((1, A_BLOCK, A_KV), lambda b, n: (b, jnp.maximum(n * WA_QB - 1, 0), c))
    cur = lambda c: pl.BlockSpec((1, WA_QB * A_BLOCK, A_KV), lambda b, n: (b, n, c))
    nxt = lambda c: pl.BlockSpec((1, A_BLOCK, A_KV),
                                 lambda b, n: (b, jnp.minimum((n + 1) * WA_QB, nb - 1), c))
    return pl.pallas_call(
        functools.partial(_wattn_kernel, seq_len=L),
        grid=(B, nb // WA_QB),
        in_specs=[pl.BlockSpec((1, WA_QB * A_BLOCK, A_Q), lambda b, n: (b, n, 0)),
                  prev(kcol), cur(kcol), nxt(kcol), prev(vcol), cur(vcol), nxt(vcol),
                  _resident(bias.shape), _resident(sink.shape), _resident(ones.shape)],
        out_specs=pl.BlockSpec((1, WA_QB * A_BLOCK, A_Q), lambda b, n: (b, n, 0)),
        out_shape=jax.ShapeDtypeStruct((B, L, A_Q), BF16),
        compiler_params=_params(2), name="window_gqa",
    )(pa, pa, pa, pa, pa, pa, pa, bias, sink, ones)


def _wattn_tables(attn_sink):
    i = np.arange(A_BLOCK)[None, :]
    j = np.arange(3 * A_BLOCK)[:, None]
    dist = np.abs(A_BLOCK + i - j)
    slopes = jnp.exp2(-8.0 * jnp.arange(1, A_HEADS + 1, dtype=F32) / A_HEADS)
    bias = -slopes[:, None, None] * jnp.asarray(dist, F32)[None]
    bias = jnp.where(jnp.asarray(dist <= A_WINDOW)[None], bias * LOG2E, NEG)
    bias = bias.reshape(A_KV_HEADS, 2, 2, 3 * A_BLOCK, A_BLOCK).transpose(0, 2, 3, 1, 4)
    bias = bias.reshape(A_KV_HEADS, 2, 3 * A_BLOCK, 2 * A_BLOCK)
    sink = jnp.repeat(attn_sink.astype(F32) * LOG2E, A_BLOCK).reshape(A_KV_HEADS, 2, 2, A_BLOCK)
    sink = sink.transpose(0, 2, 1, 3).reshape(A_KV_HEADS, 2, 1, 2 * A_BLOCK)
    return bias, sink


RET_UNROLL = 32


def _ret_kernel(lg_ref, gc_ref, q_ref, k_ref, v_ref, g_ref, o_ref, st_ref, *, nc):
    h = pl.program_id(1)
    C = R_CHUNK
    lgf, lgb = lg_ref[0, h], lg_ref[1, h]
    gcf, gcb = gc_ref[0, h], gc_ref[1, h]
    diff = (lax.broadcasted_iota(jnp.int32, (C, C), 0)
            - lax.broadcasted_iota(jnp.int32, (C, C), 1)).astype(F32)
    decay = (jnp.where(diff >= 0, jnp.exp(jnp.maximum(diff, 0.0) * lgf), 0.0)
             + jnp.where(diff <= 0, jnp.exp(jnp.maximum(-diff, 0.0) * lgb), 0.0))
    idx = lax.broadcasted_iota(jnp.int32, (C, R_DK), 0).astype(F32)
    wq2 = jnp.concatenate([jnp.exp((idx + 1.0) * lgf), jnp.exp((C - idx) * lgb)], axis=1)
    wk2 = jnp.concatenate([jnp.exp((C - 1.0 - idx) * lgf), jnp.exp(idx * lgb)], axis=1)

    def rows(n):
        return pl.ds(pl.multiple_of(n * C, C), C)

    def kv_body(i, carry):
        for u in range(RET_UNROLL):
            n = i * RET_UNROLL + u
            k = k_ref[0, rows(n), :]
            kw = (jnp.concatenate([k, k], axis=1) * wk2).astype(BF16)
            st_ref[n] = _dot_tn(kw, v_ref[0, rows(n), :].astype(BF16))
        return carry

    lax.fori_loop(0, nc // RET_UNROLL, kv_body, 0)

    def scan_body(t, carry):
        sf, sb = carry
        nb = nc - 1 - t
        kvf = st_ref[t, :R_DK, :]
        st_ref[t, :R_DK, :] = sf
        kvb = st_ref[nb, R_DK:, :]
        st_ref[nb, R_DK:, :] = sb
        return sf * gcf + kvf, sb * gcb + kvb

    zero = jnp.zeros((R_DK, R_DV), F32)
    lax.fori_loop(0, nc, scan_body, (zero, zero))

    def out_body(i, carry):
        ns = [i * RET_UNROLL + u for u in range(RET_UNROLL)]
        qs = [q_ref[0, rows(n), :] for n in ns]
        inner = [_dot_nt(q.astype(BF16), k_ref[0, rows(n), :].astype(BF16))
                 for n, q in zip(ns, qs)]
        lhs = [jnp.concatenate([s * decay, jnp.concatenate([q, q], axis=1) * wq2], axis=1).astype(BF16)
               for s, q in zip(inner, qs)]
        outs = [_dot(a, jnp.concatenate([v_ref[0, rows(n), :].astype(BF16),
                                         st_ref[n].astype(BF16)], axis=0))
                for n, a in zip(ns, lhs)]
        for n, o in zip(ns, outs):
            o = o * lax.rsqrt(jnp.mean(o * o, axis=-1, keepdims=True) + EPS)
            g = g_ref[0, rows(n), :]
            o_ref[0, rows(n), :] = (g * _sigmoid(g) * o).astype(o_ref.dtype)
        return carry

    lax.fori_loop(0, nc // RET_UNROLL, out_body, 0)


def _retention(pr, lg, gchunk):
    B, L, _ = pr.shape
    nc = L // R_CHUNK
    assert nc % RET_UNROLL == 0
    part = lambda p: pl.BlockSpec((1, L, R_DK), lambda b, h: (b, 0, p * R_HEADS + h))
    smem = pl.BlockSpec(memory_space=pltpu.SMEM)
    return pl.pallas_call(
        functools.partial(_ret_kernel, nc=nc),
        grid=(B, R_HEADS),
        in_specs=[smem, smem, part(0), part(1), part(2), part(3)],
        out_specs=pl.BlockSpec((1, L, R_DV), lambda b, h: (b, 0, h)),
        out_shape=jax.ShapeDtypeStruct((B, L, R_W), BF16),
        scratch_shapes=[pltpu.VMEM((nc, 2 * R_DK, R_DV), F32)],
        compiler_params=_params(2), name="retention",
    )(lg, gchunk, pr, pr, pr, pr)


NA_HG = 4
NA_UNROLL = 16


def _na_kernel(q_ref, k_ref, v_ref, bias_ref, o_ref, *, n_rows):
    keys = NA_ROWS * GRID_W
    head = lax.broadcasted_iota(jnp.int32, (GRID_W, NA_HG * HEAD_DIM), 1) // HEAD_DIM

    def qrows(r):
        return pl.ds(pl.multiple_of(r * GRID_W, GRID_W), GRID_W)

    def krows(r):
        rs = jnp.clip(r - NA_ROWS // 2, 0, n_rows - NA_ROWS)
        return pl.ds(pl.multiple_of(rs * GRID_W, GRID_W), keys)

    def scores(r):
        q = q_ref[0, qrows(r), :].astype(F32)
        qbd = jnp.concatenate([jnp.where(head == h, q, 0.0) for h in range(NA_HG)], axis=0).astype(BF16)
        return _dot_nt(k_ref[0, krows(r), :], qbd)

    def softmax(r, s):
        rs = jnp.clip(r - NA_ROWS // 2, 0, n_rows - NA_ROWS)
        s = s + jnp.concatenate(
            [bias_ref[0, rs - r + (NA_ROWS - 1) + j] for j in range(NA_ROWS)], axis=0)
        p = jnp.exp2(s - jnp.max(s, axis=0, keepdims=True))
        return p.astype(BF16), 1.0 / jnp.sum(p, axis=0, keepdims=True)

    def body(i, carry):
        rs_ = [i * NA_UNROLL + u for u in range(NA_UNROLL)]
        all_s = [scores(r) for r in rs_]
        all_p = [softmax(r, s) for r, s in zip(rs_, all_s)]
        all_o = [_dot_tn(p, v_ref[0, krows(r), :]) for r, (p, _) in zip(rs_, all_p)]
        for r, o, (_, inv) in zip(rs_, all_o, all_p):
            scale = jnp.transpose(jnp.broadcast_to(inv, (2 * HEAD_DIM, NA_HG * GRID_W)))
            o = o * jnp.concatenate([scale, scale], axis=1)
            out = o[:GRID_W]
            for h in range(1, NA_HG):
                out = jnp.where(head == h, o[h * GRID_W:(h + 1) * GRID_W], out)
            o_ref[0, qrows(r), :] = out.astype(o_ref.dtype)
        return carry

    lax.fori_loop(0, n_rows // NA_UNROLL, body, 0)


def _nattn(pn, bias):
    B, L, _ = pn.shape
    groups = N_HEADS // NA_HG
    assert (L // GRID_W) % NA_UNROLL == 0
    width = NA_HG * HEAD_DIM
    part = lambda p: pl.BlockSpec((1, L, width), lambda b, g: (b, 0, p * groups + g))
    return pl.pallas_call(
        functools.partial(_na_kernel, n_rows=L // GRID_W),
        grid=(B, groups),
        in_specs=[part(0), part(1), part(2),
                  pl.BlockSpec((1,) + bias.shape[1:], lambda b, g: (g, 0, 0, 0))],
        out_specs=pl.BlockSpec((1, L, width), lambda b, g: (b, 0, g)),
        out_shape=jax.ShapeDtypeStruct((B, L, N_W), BF16),
        compiler_params=_params(2), name="neighbourhood_attn",
    )(pn, pn, pn, bias)


def _na_bias(rpb):
    c = np.arange(GRID_W)
    cs = np.clip(c - NA_COLS // 2, 0, GRID_W - NA_COLS)
    valid = (c[:, None] >= cs[None, :]) & (c[:, None] < cs[None, :] + NA_COLS)
    off = c[:, None] - c[None, :] + NA_COLS - 1
    onehot = valid[None] & (off[None] == np.arange(2 * NA_COLS - 1)[:, None, None])
    t = jnp.einsum('hrd,dkc->hrkc', rpb.astype(F32) * LOG2E, jnp.asarray(onehot, F32),
                   precision=lax.Precision.HIGHEST)
    t = jnp.where(jnp.asarray(valid)[None, None], t, NEG)
    t = t.reshape(N_HEADS // NA_HG, NA_HG, 2 * NA_ROWS - 1, GRID_W, GRID_W)
    return t.transpose(0, 2, 3, 1, 4).reshape(N_HEADS // NA_HG, 2 * NA_ROWS - 1, GRID_W, NA_HG * GRID_W)


def _merge_kernel(x_ref, g_ref, ya_ref, yb_ref, yc_ref, wg_ref, wb_ref, wo_ref, o_ref):
    half = x_ref.shape[1] // 2
    parts = (slice(0, half), slice(half, 2 * half))
    merged = []
    for r in parts:
        h = _rms(x_ref[0, r, :], g_ref[...]).astype(BF16)
        acc = None
        for i, y_ref in enumerate((ya_ref, yb_ref, yc_ref)):
            gate = _sigmoid(_dot(h, wg_ref[:, i * D_MODEL:(i + 1) * D_MODEL]))
            t = gate * _dot(y_ref[0, r, :], wb_ref[i])
            acc = t if acc is None else acc + t
        merged.append(acc.astype(BF16))
    for r, m in zip(parts, merged):
        o_ref[0, r, :] = x_ref[0, r, :] + _dot(m, wo_ref[...])


def _merge(x, g, ya, yb, yc, wg, wb, wo, tm):
    B, L, _ = x.shape
    row = lambda b, i: (b, i, 0)
    y_spec = pl.BlockSpec((1, tm, BRANCH_W), row)
    return pl.pallas_call(
        _merge_kernel,
        grid=(B, L // tm),
        in_specs=[pl.BlockSpec((1, tm, D_MODEL), row), _resident((1, D_MODEL)), y_spec, y_spec, y_spec,
                  _resident(wg.shape), _resident(wb.shape), _resident(wo.shape)],
        out_specs=pl.BlockSpec((1, tm, D_MODEL), row),
        out_shape=jax.ShapeDtypeStruct(x.shape, F32),
        compiler_params=_params(2), name="merge_out",
    )(x, g, ya, yb, yc, wg, wb, wo)


def _xattn_kernel(x_ref, g_ref, wq_ref, mem_ref, gm_ref, wkv_ref, wo_ref, o_ref, kv_ref):
    @pl.when(pl.program_id(1) == 0)
    def _():
        hm = _rms(mem_ref[0], gm_ref[...]).astype(BF16)
        for c in range(0, 2 * D_MODEL, 2 * MXU_N):
            kv_ref[:, c:c + 2 * MXU_N] = _dot(hm, wkv_ref[:, c:c + 2 * MXU_N]).astype(BF16)

    x = x_ref[0]
    q = _dot(_rms(x, g_ref[...]).astype(BF16), wq_ref[...]).astype(BF16)
    cols = [slice(h * X_HEAD_DIM, (h + 1) * X_HEAD_DIM) for h in range(X_HEADS)]
    scores = [_dot_nt(q[:, c], kv_ref[:, c]) * (X_HEAD_DIM ** -0.5) for c in cols]
    probs = [jnp.exp(s - jnp.max(s, axis=-1, keepdims=True)) for s in scores]
    dens = [jnp.sum(p, axis=-1, keepdims=True) for p in probs]
    outs = [_dot(p.astype(BF16), kv_ref[:, D_MODEL + c.start:D_MODEL + c.stop])
            for p, c in zip(probs, cols)]
    attn = jnp.concatenate([(o / d).astype(BF16) for o, d in zip(outs, dens)], axis=1)
    o_ref[0] = x + _dot(attn, wo_ref[...])


def _xattn(x, g, wq, mem, g_mem, wkv, wo, tm):
    B, L, _ = x.shape
    row = lambda b, i: (b, i, 0)
    return pl.pallas_call(
        _xattn_kernel,
        grid=(B, L // tm),
        in_specs=[pl.BlockSpec((1, tm, D_MODEL), row), _resident((1, D_MODEL)), _resident(wq.shape),
                  pl.BlockSpec((1, MEM_LEN, D_MODEL), lambda b, i: (b, 0, 0)), _resident((1, D_MODEL)),
                  _resident(wkv.shape), _resident(wo.shape)],
        out_specs=pl.BlockSpec((1, tm, D_MODEL), row),
        out_shape=jax.ShapeDtypeStruct(x.shape, F32),
        scratch_shapes=[pltpu.VMEM((MEM_LEN, 2 * D_MODEL), BF16)],
        compiler_params=_params(2), name="mem_xattn",
    )(x, g, wq, mem, g_mem, wkv, wo)


def _ffn_kernel(x_ref, g_ref, wgu_ref, wd_ref, gf_ref, o_ref, act_ref, *, final_norm):
    x = x_ref[0]
    h = _rms(x, g_ref[...]).astype(BF16)
    for c in range(0, D_FF, MXU_N):
        gate = _dot(h, wgu_ref[:, c:c + MXU_N])
        up = _dot(h, wgu_ref[:, D_FF + c:D_FF + c + MXU_N])
        act_ref[:, c:c + MXU_N] = (gate * _sigmoid(gate) * up).astype(BF16)
    y = x + _dot(act_ref[...], wd_ref[...])
    if final_norm:
        y = _rms(y, gf_ref[...])
    o_ref[0] = y


def _ffn(x, g, wgu, wd, g_final, final_norm, tm):
    B, L, _ = x.shape
    row = lambda b, i: (b, i, 0)
    return pl.pallas_call(
        functools.partial(_ffn_kernel, final_norm=final_norm),
        grid=(B, L // tm),
        in_specs=[pl.BlockSpec((1, tm, D_MODEL), row), _resident((1, D_MODEL)),
                  _resident(wgu.shape), _resident(wd.shape), _resident((1, D_MODEL))],
        out_specs=pl.BlockSpec((1, tm, D_MODEL), row),
        out_shape=jax.ShapeDtypeStruct(x.shape, F32),
        scratch_shapes=[pltpu.VMEM((tm, D_FF), BF16)],
        compiler_params=_params(2), name="swiglu",
    )(x, g, wgu, wd, g_final)


CAST_STEPS = 16


def _cast_kernel(scale_ref, win_ref, *refs):
    n_others = (len(refs) - 2 * DEPTH) // (1 + DEPTH)
    others, outs = refs[:n_others], refs[n_others:]
    k = 0
    for l in range(DEPTH):
        outs[k][...] = (win_ref[l, :, :W_MIX] * scale_ref[...]).astype(BF16)
        outs[k + 1][...] = win_ref[l, :, W_MIX:].astype(BF16)
        k += 2
    for w_ref in others:
        for l in range(DEPTH):
            outs[k][...] = w_ref[l].astype(BF16)
            k += 1


def _cast_weights(col_scale, w_in, others):
    spec_in = lambda w: pl.BlockSpec((DEPTH, w.shape[1] // CAST_STEPS, w.shape[2]), lambda s: (0, s, 0))
    outs = []
    for w, cols in [(w_in, W_MIX), (w_in, w_in.shape[2] - W_MIX)] * DEPTH + [
            (w, w.shape[2]) for w in others for _ in range(DEPTH)]:
        assert w.shape[1] % (16 * CAST_STEPS) == 0
        outs.append((jax.ShapeDtypeStruct((w.shape[1], cols), BF16),
                     pl.BlockSpec((w.shape[1] // CAST_STEPS, cols), lambda s: (s, 0))))
    res = pl.pallas_call(
        _cast_kernel,
        grid=(CAST_STEPS,),
        in_specs=[_resident(col_scale.shape), spec_in(w_in)] + [spec_in(w) for w in others],
        out_specs=[o[1] for o in outs],
        out_shape=[o[0] for o in outs],
        compiler_params=_params(1), name="cast_weights",
    )(col_scale, w_in, *others)
    w_mix = [res[2 * l] for l in range(DEPTH)]
    w_gate = [res[2 * l + 1] for l in range(DEPTH)]
    rest = [[res[2 * DEPTH + i * DEPTH + l] for l in range(DEPTH)] for i in range(len(others))]
    return w_mix, w_gate, rest


TM = 1024


def _trunk(x, mem, layers, g_final):
    for l, p in enumerate(layers):
        pa, pr, pn = _proj(x, p["g_mix"], p["w_in"], TM)
        ya = _wattn(pa, p["a_bias"], p["a_sink"])
        yb = _retention(pr, p["r_lg"], p["r_gchunk"])
        yc = _nattn(pn, p["n_bias"])
        x = _merge(x, p["g_mix"], ya, yb, yc, p["w_gate"], p["w_branch"], p["w_mix_out"], TM)
        x = _xattn(x, p["g_xattn"], p["w_xq"], mem, p["g_mem"], p["w_xkv"], p["w_xo"], TM)
        x = _ffn(x, p["g_ffn"], p["w_gate_up"], p["w_down"], g_final, l == DEPTH - 1, TM)
    return x


def kernel(x_prompt, x_sample, mem_prompt, mem_sample, g_mix, w_in, attn_sink, ret_decay, na_rpb,
           w_branch, w_mix_out, g_xattn, g_mem, w_xq, w_xkv, w_xo, g_ffn, w_gate_up, w_down, g_final):
    col_scale = np.ones((W_MIX,), np.float32)
    col_scale[A_Q:A_Q + A_KV] = SCORE_SCALE
    col_scale[W_A + W_R:W_A + W_R + N_W] = SCORE_SCALE
    col_scale[W_A + R_W:W_A + 2 * R_W] = R_DK ** -0.5
    w_mix, w_gate, (wb, wmo, wxq, wxkv, wxo, wgu, wd) = _cast_weights(
        jnp.asarray(col_scale).reshape(1, W_MIX), w_in,
        [w_branch.reshape(DEPTH, 3 * BRANCH_W, D_MODEL), w_mix_out, w_xq, w_xkv, w_xo, w_gate_up, w_down])
    layers = []
    for l in range(DEPTH):
        a_bias, a_sink = _wattn_tables(attn_sink[l])
        lg = jax.nn.log_sigmoid(ret_decay[l].astype(F32))
        layers.append(dict(
            g_mix=g_mix[l].reshape(1, D_MODEL),
            w_in=w_mix[l], w_gate=w_gate[l],
            a_bias=a_bias, a_sink=a_sink,
            r_lg=lg, r_gchunk=jnp.exp(R_CHUNK * lg),
            n_bias=_na_bias(na_rpb[l]),
            w_branch=wb[l].reshape(3, BRANCH_W, D_MODEL), w_mix_out=wmo[l],
            g_xattn=g_xattn[l].reshape(1, D_MODEL), g_mem=g_mem[l].reshape(1, D_MODEL),
            w_xq=wxq[l], w_xkv=wxkv[l], w_xo=wxo[l],
            g_ffn=g_ffn[l].reshape(1, D_MODEL),
            w_gate_up=wgu[l], w_down=wd[l]))
    gf = g_final.reshape(1, D_MODEL)
    return (_trunk(x_prompt, mem_prompt, layers, gf), _trunk(x_sample, mem_sample, layers, gf))
```
